```python
import math
import jax, jax.numpy as jnp
from jax import lax
import numpy as np

D_MODEL = 2048
BATCH = 4
SEQ = 2048
DEPTH = 4

HEAD_DIM = 128
DIFF_HEADS = 4
DIFF_QK_DIM = HEAD_DIM // 2
DIFF_V_DIM = HEAD_DIM
MLA_HEADS = 6
MLA_Q_LORA = 512
MLA_KV_LORA = 512
MLA_QK_NOPE = 128
MLA_QK_ROPE = 64
MLA_V_DIM = 128
FOX_HEADS = 6
FOX_HEAD_DIM = 128
D_MIX = DIFF_HEADS * DIFF_V_DIM + MLA_HEADS * MLA_V_DIM + FOX_HEADS * FOX_HEAD_DIM
IN_SIZES = (
    DIFF_HEADS * 2 * DIFF_QK_DIM,
    DIFF_HEADS * 2 * DIFF_QK_DIM,
    DIFF_HEADS * DIFF_V_DIM,
    MLA_Q_LORA,
    MLA_KV_LORA,
    MLA_QK_ROPE,
    FOX_HEADS * FOX_HEAD_DIM,
    FOX_HEADS * FOX_HEAD_DIM,
    FOX_HEADS * FOX_HEAD_DIM,
    FOX_HEADS,
)
IN_WIDTH = 4934
D_FF = 5632
CONV_WIDTH = 3
ROPE_THETA = 500000.0
PARTIAL_ROT_DIM = DIFF_QK_DIM // 4
BLOCK_Q = 128
NORM_EPS = 1e-6
SUBLN_EPS = 1e-5
MAX_POS_OFFSET = 4096

kernel_name = 'hymba_style_diff_mla_fox_convffn_trunk'


def _rmsnorm(x, gain, eps=NORM_EPS):
    x32 = x.astype(jnp.float32)
    y = x32 * lax.rsqrt(jnp.mean(x32 * x32, axis=-1, keepdims=True) + eps)
    return (y * gain.astype(jnp.float32)).astype(x.dtype)


def _rope_tables(positions, rot_dim):
    inv_freq = ROPE_THETA ** (-jnp.arange(0, rot_dim, 2, dtype=jnp.float32) / rot_dim)
    ang = positions.astype(jnp.float32)[..., None] * inv_freq
    return jnp.cos(ang)[:, :, None, :], jnp.sin(ang)[:, :, None, :]


def _apply_rope(x, cos, sin):
    half = cos.shape[-1]
    r = 2 * half
    xr = x[..., :r].astype(jnp.float32)
    x1, x2 = xr[..., :half], xr[..., half:]
    rot = jnp.concatenate([x1 * cos - x2 * sin, x2 * cos + x1 * sin], axis=-1).astype(x.dtype)
    return jnp.concatenate([rot, x[..., r:]], axis=-1)


def _causal_probs(q_blk, k, start, scale, bias=None):
    logits = jnp.einsum('bhqd,bhkd->bhqk', q_blk, k).astype(jnp.float32) * scale
    if bias is not None:
        logits = logits + bias
    qpos = start + jnp.arange(BLOCK_Q)
    kpos = jnp.arange(k.shape[2])
    logits = jnp.where(kpos[None, :] <= qpos[:, None], logits, -jnp.inf)
    return jax.nn.softmax(logits, axis=-1)


def _sweep(block_fn, seq):
    out = lax.map(block_fn, jnp.arange(seq // BLOCK_Q))
    nb, b, h, bq, d = out.shape
    return out.transpose(1, 0, 3, 2, 4).reshape(b, nb * bq, h, d)


def _diff_attention(q, k, v, cos, sin, lam_vecs, out_gain, lambda_init):
    b, s, _ = q.shape
    q = _apply_rope(q.reshape(b, s, 2 * DIFF_HEADS, DIFF_QK_DIM), cos, sin)
    k = _apply_rope(k.reshape(b, s, 2 * DIFF_HEADS, DIFF_QK_DIM), cos, sin)
    q = q.reshape(b, s, DIFF_HEADS, 2, DIFF_QK_DIM).transpose(0, 2, 3, 1, 4)
    k = k.reshape(b, s, DIFF_HEADS, 2, DIFF_QK_DIM).transpose(0, 2, 3, 1, 4)
    v = v.reshape(b, s, DIFF_HEADS, DIFF_V_DIM).transpose(0, 2, 1, 3)
    lv = lam_vecs.astype(jnp.float32)
    lam = jnp.exp(jnp.sum(lv[0] * lv[1])) - jnp.exp(jnp.sum(lv[2] * lv[3])) + lambda_init
    scale = DIFF_QK_DIM ** -0.5

    def block(i):
        start = i * BLOCK_Q
        qb = lax.dynamic_slice_in_dim(q, start, BLOCK_Q, axis=3)
        p1 = _causal_probs(qb[:, :, 0], k[:, :, 0], start, scale)
        p2 = _causal_probs(qb[:, :, 1], k[:, :, 1], start, scale)
        return jnp.einsum('bhqk,bhkd->bhqd', (p1 - lam * p2).astype(v.dtype), v)

    o = _sweep(block, s)
    o = _rmsnorm(o, out_gain, SUBLN_EPS) * (1.0 - lambda_init)
    return o.reshape(b, s, DIFF_HEADS * DIFF_V_DIM)


def _mla_attention(c_q, c_kv, k_rope, cos, sin, q_gain, kv_gain, w_uq, w_ukv):
    b, s, _ = c_q.shape
    q = jnp.einsum('bsr,rf->bsf', _rmsnorm(c_q, q_gain), w_uq)
    q = q.reshape(b, s, MLA_HEADS, MLA_QK_NOPE + MLA_QK_ROPE)
    q = jnp.concatenate([q[..., :MLA_QK_NOPE], _apply_rope(q[..., MLA_QK_NOPE:], cos, sin)], axis=-1)
    kv = jnp.einsum('bsr,rf->bsf', _rmsnorm(c_kv, kv_gain), w_ukv)
    kv = kv.reshape(b, s, MLA_HEADS, MLA_QK_NOPE + MLA_V_DIM)
    k_r = _apply_rope(k_rope[:, :, None, :], cos, sin)
    k = jnp.concatenate([kv[..., :MLA_QK_NOPE],
                         jnp.broadcast_to(k_r, (b, s, MLA_HEADS, MLA_QK_ROPE))], axis=-1)
    v = kv[..., MLA_QK_NOPE:]
    q = q.transpose(0, 2, 1, 3)
    k = k.transpose(0, 2, 1, 3)
    v = v.transpose(0, 2, 1, 3)
    scale = (MLA_QK_NOPE + MLA_QK_ROPE) ** -0.5

    def block(i):
        start = i * BLOCK_Q
        qb = lax.dynamic_slice_in_dim(q, start, BLOCK_Q, axis=2)
        p = _causal_probs(qb, k, start, scale)
        return jnp.einsum('bhqk,bhkd->bhqd', p.astype(v.dtype), v)

    return _sweep(block, s).reshape(b, s, MLA_HEADS * MLA_V_DIM)


def _forgetting_attention(q, k, v, f_logit, f_bias):
    b, s, _ = q.shape
    q = q.reshape(b, s, FOX_HEADS, FOX_HEAD_DIM).transpose(0, 2, 1, 3)
    k = k.reshape(b, s, FOX_HEADS, FOX_HEAD_DIM).transpose(0, 2, 1, 3)
    v = v.reshape(b, s, FOX_HEADS, FOX_HEAD_DIM).transpose(0, 2, 1, 3)
    log_f = jax.nn.log_sigmoid(f_logit.astype(jnp.float32) + f_bias.astype(jnp.float32))
    cum = jnp.cumsum(log_f, axis=1).transpose(0, 2, 1)
    scale = FOX_HEAD_DIM ** -0.5

    def block(i):
        start = i * BLOCK_Q
        qb = lax.dynamic_slice_in_dim(q, start, BLOCK_Q, axis=2)
        cq = lax.dynamic_slice_in_dim(cum, start, BLOCK_Q, axis=2)
        bias = cq[..., :, None] - cum[..., None, :]
        p = _causal_probs(qb, k, start, scale, bias)
        return jnp.einsum('bhqk,bhkd->bhqd', p.astype(v.dtype), v)

    return _sweep(block, s).reshape(b, s, FOX_HEADS * FOX_HEAD_DIM)


def _conv_gated_mlp(h, w_up, conv_w, conv_b, w_down):
    u = jnp.einsum('bsd,df->bsf', h, w_up)
    u = lax.conv_general_dilated(u, conv_w[:, None, :], window_strides=(1,),
                                 padding=[(CONV_WIDTH - 1, 0)],
                                 dimension_numbers=('NWC', 'WIO', 'NWC'),
                                 feature_group_count=u.shape[-1]) + conv_b
    a, g = jnp.split(u, 2, axis=-1)
    return jnp.einsum('bsf,fd->bsd', jax.nn.silu(g) * a, w_down)


def setup_inputs(seed: int = 0) -> dict:
    key = jax.random.key(seed)
    ks = jax.random.split(key, 20)

    def nrm(k, shape, scale):
        return jax.random.normal(k, shape, jnp.float32) * scale

    def gain(k, shape):
        return 1.0 + 0.05 * jax.random.normal(k, shape, jnp.float32)

    x = nrm(ks[0], (BATCH, SEQ, D_MODEL), 1.0)
    offsets = jax.random.randint(ks[1], (BATCH, 1), 0, MAX_POS_OFFSET, dtype=jnp.int32)
    positions = (jnp.arange(SEQ, dtype=jnp.int32)[None, :] + offsets).astype(jnp.int32)
    return {
        'x': x,
        'positions': positions,
        'attn_norm': gain(ks[2], (DEPTH, D_MODEL)),
        'w_in': nrm(ks[3], (DEPTH, D_MODEL, IN_WIDTH), D_MODEL ** -0.5),
        'diff_lambda': nrm(ks[4], (DEPTH, 4, DIFF_QK_DIM), 0.1),
        'diff_out_norm': gain(ks[5], (DEPTH, DIFF_V_DIM)),
        'mla_q_norm': gain(ks[6], (DEPTH, MLA_Q_LORA)),
        'mla_kv_norm': gain(ks[7], (DEPTH, MLA_KV_LORA)),
        'mla_w_uq': nrm(ks[8], (DEPTH, MLA_Q_LORA, MLA_HEADS * (MLA_QK_NOPE + MLA_QK_ROPE)), MLA_Q_LORA ** -0.5),
        'mla_w_ukv': nrm(ks[9], (DEPTH, MLA_KV_LORA, MLA_HEADS * (MLA_QK_NOPE + MLA_V_DIM)), MLA_KV_LORA ** -0.5),
        'fox_forget_bias': jax.random.uniform(ks[10], (DEPTH, FOX_HEADS), jnp.float32, 1.0, 4.0),
        'w_o': nrm(ks[11], (DEPTH, D_MIX, D_MODEL), D_MIX ** -0.5),
        'ffn_norm': gain(ks[12], (DEPTH, D_MODEL)),
        'ffn_w_up': nrm(ks[13], (DEPTH, D_MODEL, 2 * D_FF), D_MODEL ** -0.5),
        'ffn_conv_w': nrm(ks[14], (DEPTH, CONV_WIDTH, 2 * D_FF), CONV_WIDTH ** -0.5),
        'ffn_conv_b': nrm(ks[15], (DEPTH, 2 * D_FF), 0.02),
        'ffn_w_down': nrm(ks[16], (DEPTH, D_FF, D_MODEL), D_FF ** -0.5),
        'final_norm': gain(ks[17], (D_MODEL,)),
    }


def reference(x, positions, attn_norm, w_in, diff_lambda, diff_out_norm, mla_q_norm, mla_kv_norm,
              mla_w_uq, mla_w_ukv, fox_forget_bias, w_o, ffn_norm, ffn_w_up, ffn_conv_w, ffn_conv_b,
              ffn_w_down, final_norm):
    cos_p, sin_p = _rope_tables(positions, PARTIAL_ROT_DIM)
    cos_m, sin_m = _rope_tables(positions, MLA_QK_ROPE)
    split_points = [int(v) for v in np.cumsum(IN_SIZES)[:-1]]
    for l in range(DEPTH):
        lambda_init = 0.8 - 0.6 * math.exp(-0.3 * l)
        h = _rmsnorm(x, attn_norm[l])
        z = jnp.einsum('bsd,de->bse', h, w_in[l])
        (a_q, a_k, a_v, m_cq, m_ckv, m_kr, f_q, f_k, f_v, f_g) = jnp.split(z, split_points, axis=-1)
        o_a = _diff_attention(a_q, a_k, a_v, cos_p, sin_p, diff_lambda[l], diff_out_norm[l], lambda_init)
        o_b = _mla_attention(m_cq, m_ckv, m_kr, cos_m, sin_m, mla_q_norm[l], mla_kv_norm[l],
                             mla_w_uq[l], mla_w_ukv[l])
        o_c = _forgetting_attention(f_q, f_k, f_v, f_g, fox_forget_bias[l])
        mix = jnp.concatenate([o_a, o_b, o_c], axis=-1)
        x = x + jnp.einsum('bsm,md->bsd', mix, w_o[l])
        h = _rmsnorm(x, ffn_norm[l])
        x = x + _conv_gated_mlp(h, ffn_w_up[l], ffn_conv_w[l], ffn_conv_b[l], ffn_w_down[l])
    return _rmsnorm(x, final_norm)
```

```python
import functools
import math

import numpy as np
import jax
import jax.numpy as jnp
from jax import lax
from jax.experimental import pallas as pl
from jax.experimental.pallas import tpu as pltpu

HEAD_DIM = 128
DIFF_HEADS = 4
DIFF_QK_DIM = 64
MLA_HEADS = 6
MLA_LORA = 512
MLA_QK_NOPE = 128
MLA_QK_ROPE = 64
FOX_HEADS = 6
D_FF = 5632
ROPE_THETA = 500000.0
PARTIAL_ROT_DIM = DIFF_QK_DIM // 4
NORM_EPS = 1e-6
SUBLN_EPS = 1e-5

LANES = 128
SUBLANES = 8
VMEM_LIMIT_BYTES = 56 * 1024 * 1024

Z_AQ, Z_AK, Z_AV = 0, 512, 1024
Z_CQ, Z_CKV, Z_KR = 1536, 2048, 2560
Z_FQ, Z_FK, Z_FV = 2688, 3456, 4224
Z_FG = 4992
Z_WIDTH = 5120
MLA_QK_PAD = 256

TM_PROJ = 1024
TN_IN = 1024
TN_OUT = 1024
TN_FF = 512
TN_DOWN = 512
TM_PREP = 512
TQ = 256


def _cparams(sem):
    return pltpu.CompilerParams(dimension_semantics=sem, vmem_limit_bytes=VMEM_LIMIT_BYTES)


def _rms_scale(x32, eps):
    return lax.rsqrt(jnp.mean(x32 * x32, axis=-1, keepdims=True) + eps)


def _rope_table_kernel(pos_ref, invf_ref, m1_ref, m2_ref, c_ref, s1_ref, s2_ref):
    ang = pos_ref[...].astype(jnp.float32) * invf_ref[...]
    sn = jnp.sin(ang)
    c_ref[...] = jnp.cos(ang)
    s1_ref[...] = sn * m1_ref[...]
    s2_ref[...] = sn * m2_ref[...]


def _rope_tables(pos_col, invf, m1, m2):
    m = pos_col.shape[0]
    row = pl.BlockSpec((1, LANES), lambda i: (0, 0))
    tab = pl.BlockSpec((TM_PREP, LANES), lambda i: (i, 0))
    return pl.pallas_call(
        _rope_table_kernel,
        grid=(m // TM_PREP,),
        in_specs=[pl.BlockSpec((TM_PREP, 1), lambda i: (i, 0)), row, row, row],
        out_specs=[tab, tab, tab],
        out_shape=[jax.ShapeDtypeStruct((m, LANES), jnp.float32)] * 3,
        compiler_params=_cparams(("parallel",)),
        name="rope_tables",
    )(pos_col, invf, m1, m2)


def _rope_lane_patterns():
    lane = np.arange(LANES)
    half_a = PARTIAL_ROT_DIM // 2
    r = lane % DIFF_QK_DIM
    inv_a = ROPE_THETA ** (-jnp.arange(0, PARTIAL_ROT_DIM, 2, dtype=jnp.float32) / PARTIAL_ROT_DIM)
    invf_a = jnp.where(jnp.asarray(r < PARTIAL_ROT_DIM), inv_a[jnp.asarray(r % half_a)], 0.0)
    m1_a = np.where(r < half_a, -1.0, 0.0)
    m2_a = np.where((r >= half_a) & (r < PARTIAL_ROT_DIM), 1.0, 0.0)
    half_b = MLA_QK_ROPE // 2
    inv_b = ROPE_THETA ** (-jnp.arange(0, MLA_QK_ROPE, 2, dtype=jnp.float32) / MLA_QK_ROPE)
    invf_b = jnp.where(jnp.asarray(lane < MLA_QK_ROPE), inv_b[jnp.asarray(lane % half_b)], 0.0)
    m1_b = np.where(lane < half_b, -1.0, 0.0)
    m2_b = np.where((lane >= half_b) & (lane < MLA_QK_ROPE), 1.0, 0.0)
    f = lambda a: jnp.asarray(a, jnp.float32).reshape(1, LANES)
    return (f(invf_a), f(m1_a), f(m2_a)), (f(invf_b), f(m1_b), f(m2_b))


def _rope_block(x, c, s1, s2, d):
    return x * c + pltpu.roll(x, LANES - d, 1) * s1 + pltpu.roll(x, d, 1) * s2


def _in_proj_kernel(x_ref, g_ref, w_ref, z_ref, zg_ref, h_ref):
    j = pl.program_id(1)

    @pl.when(j == 0)
    def _():
        x = x_ref[...]
        h_ref[...] = (x * _rms_scale(x, NORM_EPS) * g_ref[...]).astype(jnp.bfloat16)

    acc = jnp.dot(h_ref[...], w_ref[...], preferred_element_type=jnp.float32)
    z_ref[...] = acc.astype(jnp.bfloat16)

    @pl.when(j == pl.num_programs(1) - 1)
    def _():
        zg_ref[...] = acc[:, TN_IN - LANES:]


def _in_proj(x, gain, w):
    m, d = x.shape
    return pl.pallas_call(
        _in_proj_kernel,
        grid=(m // TM_PROJ, Z_WIDTH // TN_IN),
        in_specs=[pl.BlockSpec((TM_PROJ, d), lambda i, j: (i, 0)),
                  pl.BlockSpec((1, d), lambda i, j: (0, 0)),
                  pl.BlockSpec((d, TN_IN), lambda i, j: (0, j))],
        out_specs=[pl.BlockSpec((TM_PROJ, TN_IN), lambda i, j: (i, j)),
                   pl.BlockSpec((TM_PROJ, LANES), lambda i, j: (i, 0))],
        out_shape=[jax.ShapeDtypeStruct((m, Z_WIDTH), jnp.bfloat16),
                   jax.ShapeDtypeStruct((m, LANES), jnp.float32)],
        scratch_shapes=[pltpu.VMEM((TM_PROJ, d), jnp.bfloat16)],
        compiler_params=_cparams(("parallel", "arbitrary")),
        name="in_proj",
    )(x, gain, w)


def _diff_rope_kernel(q_ref, k_ref, c_ref, s1_ref, s2_ref, o_ref):
    c, s1, s2 = c_ref[...], s1_ref[...], s2_ref[...]
    half = PARTIAL_ROT_DIM // 2
    scale = DIFF_QK_DIM ** -0.5
    nblk = q_ref.shape[1] // LANES
    for b in range(nblk):
        sl = slice(b * LANES, (b + 1) * LANES)
        q = _rope_block(q_ref[:, sl].astype(jnp.float32), c, s1, s2, half)
        o_ref[:, sl] = (q * scale).astype(jnp.bfloat16)
        k = _rope_block(k_ref[:, sl].astype(jnp.float32), c, s1, s2, half)
        o_ref[:, nblk * LANES + b * LANES: nblk * LANES + (b + 1) * LANES] = k.astype(jnp.bfloat16)


def _diff_rope(z, tabs):
    m = z.shape[0]
    w = DIFF_HEADS * 2 * DIFF_QK_DIM
    tab = pl.BlockSpec((TM_PREP, LANES), lambda i: (i, 0))
    return pl.pallas_call(
        _diff_rope_kernel,
        grid=(m // TM_PREP,),
        in_specs=[pl.BlockSpec((TM_PREP, w), lambda i: (i, Z_AQ // w)),
                  pl.BlockSpec((TM_PREP, w), lambda i: (i, Z_AK // w)),
                  tab, tab, tab],
        out_specs=pl.BlockSpec((TM_PREP, 2 * w), lambda i: (i, 0)),
        out_shape=jax.ShapeDtypeStruct((m, 2 * w), jnp.bfloat16),
        compiler_params=_cparams(("parallel",)),
        name="diff_rope",
    )(z, z, *tabs)


def _mla_up_kernel(cq_ref, ckv_ref, kr_ref, gq_ref, gkv_ref, wq_ref, wk_ref, wv_ref,
                   c_ref, s1_ref, s2_ref, q_ref, k_ref, v_ref):
    c, s1, s2 = c_ref[...], s1_ref[...], s2_ref[...]
    half = MLA_QK_ROPE // 2
    scale = (MLA_QK_NOPE + MLA_QK_ROPE) ** -0.5

    cq = cq_ref[...].astype(jnp.float32)
    cqn = (cq * _rms_scale(cq, NORM_EPS) * gq_ref[...]).astype(jnp.bfloat16)
    q = jnp.dot(cqn, wq_ref[...], preferred_element_type=jnp.float32)
    ckv = ckv_ref[...].astype(jnp.float32)
    ckvn = (ckv * _rms_scale(ckv, NORM_EPS) * gkv_ref[...]).astype(jnp.bfloat16)
    kn = jnp.dot(ckvn, wk_ref[...], preferred_element_type=jnp.float32)
    v_ref[...] = jnp.dot(ckvn, wv_ref[...], preferred_element_type=jnp.float32).astype(jnp.bfloat16)
    kr = _rope_block(kr_ref[...].astype(jnp.float32), c, s1, s2, half).astype(jnp.bfloat16)

    for h in range(MLA_HEADS):
        base = h * MLA_QK_PAD
        q_ref[:, base:base + LANES] = (q[:, base:base + LANES] * scale).astype(jnp.bfloat16)
        qr = _rope_block(q[:, base + LANES:base + 2 * LANES], c, s1, s2, half)
        q_ref[:, base + LANES:base + 2 * LANES] = (qr * scale).astype(jnp.bfloat16)
        k_ref[:, base:base + LANES] = kn[:, h * LANES:(h + 1) * LANES].astype(jnp.bfloat16)
        k_ref[:, base + LANES:base + 2 * LANES] = kr


def _mla_up(z, gq, gkv, wq, wk, wv, tabs):
    m = z.shape[0]
    hq = MLA_HEADS * MLA_QK_PAD
    hv = MLA_HEADS * HEAD_DIM
    tab = pl.BlockSpec((TM_PREP, LANES), lambda i: (i, 0))
    full = lambda shp: pl.BlockSpec(shp, lambda i: (0, 0))
    return pl.pallas_call(
        _mla_up_kernel,
        grid=(m // TM_PREP,),
        in_specs=[pl.BlockSpec((TM_PREP, MLA_LORA), lambda i: (i, Z_CQ // MLA_LORA)),
                  pl.BlockSpec((TM_PREP, MLA_LORA), lambda i: (i, Z_CKV // MLA_LORA)),
                  pl.BlockSpec((TM_PREP, LANES), lambda i: (i, Z_KR // LANES)),
                  full((1, MLA_LORA)), full((1, MLA_LORA)),
                  full((MLA_LORA, hq)), full((MLA_LORA, hv)), full((MLA_LORA, hv)),
                  tab, tab, tab],
        out_specs=[pl.BlockSpec((TM_PREP, hq), lambda i: (i, 0)),
                   pl.BlockSpec((TM_PREP, hq), lambda i: (i, 0)),
                   pl.BlockSpec((TM_PREP, hv), lambda i: (i, 0))],
        out_shape=[jax.ShapeDtypeStruct((m, hq), jnp.bfloat16),
                   jax.ShapeDtypeStruct((m, hq), jnp.bfloat16),
                   jax.ShapeDtypeStruct((m, hv), jnp.bfloat16)],
        compiler_params=_cparams(("parallel",)),
        name="mla_up",
    )(z, z, z, gq, gkv, wq, wk, wv, *tabs)


def _fox_gate_kernel(zg_ref, fb_ref, col_ref, row_ref):
    x = zg_ref[...] + fb_ref[...]
    logf = jnp.minimum(x, 0.0) - jnp.log(1.0 + jnp.exp(-jnp.abs(x)))
    s = logf.shape[0]
    row = lax.broadcasted_iota(jnp.int32, logf.shape, 0)
    d = 1
    while d < s:
        logf = logf + jnp.where(row >= d, pltpu.roll(logf, d, 0), 0.0)
        d *= 2
    col_ref[...] = logf
    row_ref[0] = logf.T[:SUBLANES, :]


def _fox_gate(zg, fb, batch, seq):
    return pl.pallas_call(
        _fox_gate_kernel,
        grid=(batch,),
        in_specs=[pl.BlockSpec((seq, LANES), lambda b: (b, 0)),
                  pl.BlockSpec((1, LANES), lambda b: (0, 0))],
        out_specs=[pl.BlockSpec((seq, LANES), lambda b: (b, 0)),
                   pl.BlockSpec((1, SUBLANES, seq), lambda b: (b, 0, 0))],
        out_shape=[jax.ShapeDtypeStruct((batch * seq, LANES), jnp.float32),
                   jax.ShapeDtypeStruct((batch, SUBLANES, seq), jnp.float32)],
        compiler_params=_cparams(("parallel",)),
        name="fox_gate",
    )(zg, fb)


_NT = (((1,), (1,)), ((), ()))


def _softmax_tile(q, k_ref, s_ref, p_ref, i, bias_fn=None):
    tq = q.shape[0]
    mpart = jnp.full((tq, LANES), -jnp.inf, jnp.float32)
    for c in range(i + 1):
        cs = slice(c * TQ, (c + 1) * TQ)
        s = lax.dot_general(q, k_ref[cs, :], _NT, preferred_element_type=jnp.float32)
        if bias_fn is not None:
            s = s + bias_fn(cs)
        if c == i:
            rows = lax.broadcasted_iota(jnp.int32, s.shape, 0)
            cols = lax.broadcasted_iota(jnp.int32, s.shape, 1)
            s = jnp.where(cols <= rows, s, -jnp.inf)
        s_ref[:, cs] = s
        for b in range(TQ // LANES):
            mpart = jnp.maximum(mpart, s[:, b * LANES:(b + 1) * LANES])
    m = jnp.max(mpart, axis=1, keepdims=True)
    lpart = jnp.zeros((tq, LANES), jnp.float32)
    for c in range(i + 1):
        cs = slice(c * TQ, (c + 1) * TQ)
        p = jnp.exp(s_ref[:, cs] - m)
        for b in range(TQ // LANES):
            lpart = lpart + p[:, b * LANES:(b + 1) * LANES]
        p_ref[:, cs] = p.astype(jnp.bfloat16)
    return jnp.sum(lpart, axis=1, keepdims=True)


def _pv(p_ref, v_ref, i):
    n = (i + 1) * TQ
    return jnp.dot(p_ref[:, :n], v_ref[:n, :], preferred_element_type=jnp.float32)


def _mla_attn_kernel(q_ref, k_ref, v_ref, o_ref, s_ref, p_ref):
    for i in range(q_ref.shape[0] // TQ):
        rs = slice(i * TQ, (i + 1) * TQ)
        l = _softmax_tile(q_ref[rs, :], k_ref, s_ref, p_ref, i)
        o_ref[rs, :] = (_pv(p_ref, v_ref, i) / l).astype(o_ref.dtype)


def _fox_attn_kernel(q_ref, k_ref, v_ref, ccol_ref, crow_ref, o_ref, s_ref, p_ref):
    h = pl.program_id(1)
    scale = HEAD_DIM ** -0.5
    crow = crow_ref[0, pl.ds(h, 1), :]
    for i in range(q_ref.shape[0] // TQ):
        rs = slice(i * TQ, (i + 1) * TQ)
        q = (q_ref[rs, :].astype(jnp.float32) * scale).astype(jnp.bfloat16)
        cc = ccol_ref[rs, :]
        lane = lax.broadcasted_iota(jnp.int32, cc.shape, 1)
        cq = jnp.sum(jnp.where(lane == h, cc, 0.0), axis=1, keepdims=True)
        l = _softmax_tile(q, k_ref, s_ref, p_ref, i, bias_fn=lambda cs: cq - crow[:, cs])
        o_ref[rs, :] = (_pv(p_ref, v_ref, i) / l).astype(o_ref.dtype)


def _diff_attn_kernel(lambda_init, q_ref, k_ref, v_ref, lam_ref, g_ref, o_ref, s_ref, p_ref):
    lv = lam_ref[...]
    lam = (jnp.exp(jnp.sum(lv[0:1] * lv[1:2], axis=1, keepdims=True))
           - jnp.exp(jnp.sum(lv[2:3] * lv[3:4], axis=1, keepdims=True)) + lambda_init)
    gain = g_ref[...] * (1.0 - lambda_init)
    for i in range(q_ref.shape[0] // TQ):
        rs = slice(i * TQ, (i + 1) * TQ)
        q = q_ref[rs, :]
        lane = lax.broadcasted_iota(jnp.int32, q.shape, 1)
        zero = jnp.zeros_like(q)
        q1 = jnp.where(lane < DIFF_QK_DIM, q, zero)
        q2 = jnp.where(lane >= DIFF_QK_DIM, q, zero)
        l1 = _softmax_tile(q1, k_ref, s_ref, p_ref, i)
        o1 = _pv(p_ref, v_ref, i) / l1
        l2 = _softmax_tile(q2, k_ref, s_ref, p_ref, i)
        o2 = _pv(p_ref, v_ref, i) / l2
        o = o1 - lam * o2
        o_ref[rs, :] = (o * _rms_scale(o, SUBLN_EPS) * gain).astype(o_ref.dtype)


def _attn_call(kernel, name, batch, seq, heads, ins, in_specs):
    return pl.pallas_call(
        kernel,
        grid=(batch, heads),
        in_specs=in_specs,
        out_specs=pl.BlockSpec((seq, HEAD_DIM), lambda b, h: (b, h)),
        out_shape=jax.ShapeDtypeStruct((batch * seq, heads * HEAD_DIM), jnp.bfloat16),
        scratch_shapes=[pltpu.VMEM((TQ, seq), jnp.float32), pltpu.VMEM((TQ, seq), jnp.bfloat16)],
        compiler_params=_cparams(("parallel", "parallel")),
        name=name,
    )(*ins)


def _colblk(seq, width, col0):
    assert col0 % width == 0
    return pl.BlockSpec((seq, width), lambda b, h: (b, col0 // width + h))


def _diff_attn(qk, z, lam, gain, lambda_init, batch, seq):
    kern = functools.partial(_diff_attn_kernel, lambda_init)
    specs = [_colblk(seq, HEAD_DIM, 0), _colblk(seq, HEAD_DIM, DIFF_HEADS * HEAD_DIM),
             _colblk(seq, HEAD_DIM, Z_AV),
             pl.BlockSpec((4, DIFF_QK_DIM), lambda b, h: (0, 0)),
             pl.BlockSpec((1, HEAD_DIM), lambda b, h: (0, 0))]
    return _attn_call(kern, "diff_attn", batch, seq, DIFF_HEADS, (qk, qk, z, lam, gain), specs)


def _mla_attn(q, k, v, batch, seq):
    specs = [_colblk(seq, MLA_QK_PAD, 0), _colblk(seq, MLA_QK_PAD, 0), _colblk(seq, HEAD_DIM, 0)]
    return _attn_call(_mla_attn_kernel, "mla_attn", batch, seq, MLA_HEADS, (q, k, v), specs)


def _fox_attn(z, ccol, crow, batch, seq):
    specs = [_colblk(seq, HEAD_DIM, Z_FQ), _colblk(seq, HEAD_DIM, Z_FK), _colblk(seq, HEAD_DIM, Z_FV),
             pl.BlockSpec((seq, LANES), lambda b, h: (b, 0)),
             pl.BlockSpec((1, SUBLANES, seq), lambda b, h: (b, 0, 0))]
    return _attn_call(_fox_attn_kernel, "fox_attn", batch, seq, FOX_HEADS, (z, z, z, ccol, crow), specs)


def _out_proj_kernel(x_ref, a_ref, b_ref, c_ref, wa_ref, wb_ref, wc_ref, o_ref):
    acc = jnp.dot(a_ref[...], wa_ref[...], preferred_element_type=jnp.float32)
    acc = acc + jnp.dot(b_ref[...], wb_ref[...], preferred_element_type=jnp.float32)
    acc = acc + jnp.dot(c_ref[...], wc_ref[...], preferred_element_type=jnp.float32)
    o_ref[...] = x_ref[...] + acc


def _out_proj(x, oa, ob, oc, wa, wb, wc):
    m, d = x.shape
    lhs = lambda a: pl.BlockSpec((TM_PROJ, a.shape[1]), lambda i, j: (i, 0))
    rhs = lambda w: pl.BlockSpec((w.shape[0], TN_OUT), lambda i, j: (0, j))
    xo = pl.BlockSpec((TM_PROJ, TN_OUT), lambda i, j: (i, j))
    return pl.pallas_call(
        _out_proj_kernel,
        grid=(m // TM_PROJ, d // TN_OUT),
        in_specs=[xo, lhs(oa), lhs(ob), lhs(oc), rhs(wa), rhs(wb), rhs(wc)],
        out_specs=xo,
        out_shape=jax.ShapeDtypeStruct((m, d), jnp.float32),
        compiler_params=_cparams(("parallel", "parallel")),
        name="out_proj",
    )(x, oa, ob, oc, wa, wb, wc)


def _ffn_up_kernel(tiles_per_seq, x_ref, g_ref, wa_ref, wg_ref, cwa_ref, cwg_ref, cba_ref, cbg_ref,
                   o_ref, h_ref, ua_ref, ug_ref, carry_ref):
    i, j = pl.program_id(0), pl.program_id(1)
    tm = x_ref.shape[0]

    @pl.when(j == 0)
    def _():
        x = x_ref[...]
        h_ref[...] = (x * _rms_scale(x, NORM_EPS) * g_ref[...]).astype(jnp.bfloat16)

    @pl.when(i % tiles_per_seq == 0)
    def _():
        carry_ref[j] = jnp.zeros(carry_ref.shape[1:], jnp.float32)

    def conv(u_ref, w_ref, half, cw_ref, cb_ref):
        u_ref[0:SUBLANES, :] = carry_ref[j, half]
        u = jnp.dot(h_ref[...], w_ref[...], preferred_element_type=jnp.float32)
        u_ref[SUBLANES:, :] = u
        carry_ref[j, half] = u[tm - SUBLANES:, :]
        cw = cw_ref[...]
        return (u * cw[2:3] + u_ref[SUBLANES - 1:SUBLANES - 1 + tm, :] * cw[1:2]
                + u_ref[SUBLANES - 2:SUBLANES - 2 + tm, :] * cw[0:1] + cb_ref[...])

    a = conv(ua_ref, wa_ref, 0, cwa_ref, cba_ref)
    g = conv(ug_ref, wg_ref, 1, cwg_ref, cbg_ref)
    o_ref[...] = (g / (1.0 + jnp.exp(-g)) * a).astype(o_ref.dtype)


def _ffn_up(x, gain, w_up, conv_w, conv_b, seq):
    m, d = x.shape
    nj = D_FF // TN_FF
    kern = functools.partial(_ffn_up_kernel, seq // TM_PROJ)
    col = lambda rows, off: pl.BlockSpec((rows, TN_FF), lambda i, j: (0, j + off))
    return pl.pallas_call(
        kern,
        grid=(m // TM_PROJ, nj),
        in_specs=[pl.BlockSpec((TM_PROJ, d), lambda i, j: (i, 0)),
                  pl.BlockSpec((1, d), lambda i, j: (0, 0)),
                  col(d, 0), col(d, nj), col(3, 0), col(3, nj), col(1, 0), col(1, nj)],
        out_specs=pl.BlockSpec((TM_PROJ, TN_FF), lambda i, j: (i, j)),
        out_shape=jax.ShapeDtypeStruct((m, D_FF), jnp.bfloat16),
        scratch_shapes=[pltpu.VMEM((TM_PROJ, d), jnp.bfloat16),
                        pltpu.VMEM((TM_PROJ + SUBLANES, TN_FF), jnp.float32),
                        pltpu.VMEM((TM_PROJ + SUBLANES, TN_FF), jnp.float32),
                        pltpu.VMEM((nj, 2, SUBLANES, TN_FF), jnp.float32)],
        compiler_params=_cparams(("arbitrary", "arbitrary")),
        name="ffn_up",
    )(x, gain, w_up, w_up, conv_w, conv_w, conv_b, conv_b)


def _ffn_down_kernel(x_ref, a_ref, w_ref, o_ref):
    o_ref[...] = x_ref[...] + jnp.dot(a_ref[...], w_ref[...], preferred_element_type=jnp.float32)


def _ffn_down(x, act, w):
    m, d = x.shape
    k = act.shape[1]
    xo = pl.BlockSpec((TM_PROJ, TN_DOWN), lambda i, j: (i, j))
    return pl.pallas_call(
        _ffn_down_kernel,
        grid=(m // TM_PROJ, d // TN_DOWN),
        in_specs=[xo, pl.BlockSpec((TM_PROJ, k), lambda i, j: (i, 0)),
                  pl.BlockSpec((k, TN_DOWN), lambda i, j: (0, j))],
        out_specs=xo,
        out_shape=jax.ShapeDtypeStruct((m, d), jnp.float32),
        compiler_params=_cparams(("parallel", "parallel")),
        name="ffn_down",
    )(x, act, w)


def _final_norm_kernel(x_ref, g_ref, o_ref):
    x = x_ref[...]
    o_ref[...] = x * _rms_scale(x, NORM_EPS) * g_ref[...]


def _final_norm(x, gain):
    m, d = x.shape
    blk = pl.BlockSpec((TM_PREP, d), lambda i: (i, 0))
    return pl.pallas_call(
        _final_norm_kernel,
        grid=(m // TM_PREP,),
        in_specs=[blk, pl.BlockSpec((1, d), lambda i: (0, 0))],
        out_specs=blk,
        out_shape=jax.ShapeDtypeStruct((m, d), jnp.float32),
        compiler_params=_cparams(("parallel",)),
        name="final_norm",
    )(x, gain)


def _pad_w_in(w):
    d = w.shape[0]
    kr_end = Z_KR + MLA_QK_ROPE
    parts = [w[:, :kr_end], jnp.zeros((d, Z_FQ - kr_end), w.dtype),
             w[:, kr_end:], jnp.zeros((d, Z_WIDTH - Z_FQ - (w.shape[1] - kr_end)), w.dtype)]
    return jnp.concatenate(parts, axis=1).astype(jnp.bfloat16)


def _pad_w_uq(w):
    r = w.shape[0]
    w = w.reshape(r, MLA_HEADS, MLA_QK_NOPE + MLA_QK_ROPE)
    w = jnp.pad(w, ((0, 0), (0, 0), (0, MLA_QK_PAD - MLA_QK_NOPE - MLA_QK_ROPE)))
    return w.reshape(r, MLA_HEADS * MLA_QK_PAD).astype(jnp.bfloat16)


def _split_w_ukv(w):
    r = w.shape[0]
    w = w.reshape(r, MLA_HEADS, MLA_QK_NOPE + HEAD_DIM)
    wk = w[:, :, :MLA_QK_NOPE].reshape(r, MLA_HEADS * MLA_QK_NOPE)
    wv = w[:, :, MLA_QK_NOPE:].reshape(r, MLA_HEADS * HEAD_DIM)
    return wk.astype(jnp.bfloat16), wv.astype(jnp.bfloat16)


def kernel(x, positions, attn_norm, w_in, diff_lambda, diff_out_norm, mla_q_norm, mla_kv_norm, mla_w_uq,
           mla_w_ukv, fox_forget_bias, w_o, ffn_norm, ffn_w_up, ffn_conv_w, ffn_conv_b, ffn_w_down,
           final_norm):
    batch, seq, d = x.shape
    depth = w_in.shape[0]
    m = batch * seq
    assert seq % TQ == 0 and seq % TM_PROJ == 0 and m % TM_PROJ == 0

    pat_a, pat_b = _rope_lane_patterns()
    pos_col = positions.reshape(m, 1)
    tabs_a = _rope_tables(pos_col, *pat_a)
    tabs_b = _rope_tables(pos_col, *pat_b)

    row = lambda a: a.reshape(1, -1)
    n_a, n_b = DIFF_HEADS * HEAD_DIM, MLA_HEADS * HEAD_DIM
    xf = x.reshape(m, d)
    for l in range(depth):
        lambda_init = 0.8 - 0.6 * math.exp(-0.3 * l)
        z, zg = _in_proj(xf, row(attn_norm[l]), _pad_w_in(w_in[l]))
        qk_a = _diff_rope(z, tabs_a)
        wk, wv = _split_w_ukv(mla_w_ukv[l])
        q_b, k_b, v_b = _mla_up(z, row(mla_q_norm[l]), row(mla_kv_norm[l]), _pad_w_uq(mla_w_uq[l]),
                                wk, wv, tabs_b)
        fb = jnp.pad(fox_forget_bias[l], (0, LANES - FOX_HEADS)).reshape(1, LANES)
        ccol, crow = _fox_gate(zg, fb, batch, seq)
        o_a = _diff_attn(qk_a, z, diff_lambda[l], row(diff_out_norm[l]), lambda_init, batch, seq)
        o_b = _mla_attn(q_b, k_b, v_b, batch, seq)
        o_c = _fox_attn(z, ccol, crow, batch, seq)
        wo = w_o[l].astype(jnp.bfloat16)
        xf = _out_proj(xf, o_a, o_b, o_c, wo[:n_a], wo[n_a:n_a + n_b], wo[n_a + n_b:])
        act = _ffn_up(xf, row(ffn_norm[l]), ffn_w_up[l].astype(jnp.bfloat16), ffn_conv_w[l],
                      row(ffn_conv_b[l]), seq)
        xf = _ffn_down(xf, act, ffn_w_down[l].astype(jnp.bfloat16))
    return _final_norm(xf, row(final_norm)).reshape(batch, seq, d)
```

```python
import functools
import math

import numpy as np
import jax
import jax.numpy as jnp
from jax import lax
from jax.experimental import pallas as pl
from jax.experimental.pallas import tpu as pltpu

HEAD_DIM = 128
DIFF_HEADS = 4
DIFF_QK_DIM = 64
MLA_HEADS = 6
MLA_LORA = 512
MLA_QK_NOPE = 128
MLA_QK_ROPE = 64
FOX_HEADS = 6
D_FF = 5632
ROPE_THETA = 500000.0
PARTIAL_ROT_DIM = DIFF_QK_DIM // 4
NORM_EPS = 1e-6
SUBLN_EPS = 1e-5

LANES = 128
SUBLANES = 8
VMEM_LIMIT_BYTES = 56 * 1024 * 1024

Z_AQ, Z_AK, Z_AV = 0, 512, 1024
Z_CQ, Z_CKV, Z_KR = 1536, 2048, 2560
Z_FQ, Z_FK, Z_FV = 2624, 3392, 4160
Z_FG = 4928
IN_WIDTH = Z_FG + FOX_HEADS
FOX_SHIFT = Z_FQ % LANES
assert Z_FK % LANES == FOX_SHIFT and Z_FV % LANES == FOX_SHIFT and Z_FG % LANES == FOX_SHIFT
MLA_QK_PAD = 256

TM_PROJ = 1024
TN_IN = 1024
TN_OUT = 1024
TN_FF = 512
TN_DOWN = 512
TM_PREP = 512
TQ = 256
Z_WIDTH = -(-IN_WIDTH // TN_IN) * TN_IN


def _cparams(sem):
    return pltpu.CompilerParams(dimension_semantics=sem, vmem_limit_bytes=VMEM_LIMIT_BYTES)


def _rms_scale(x32, eps):
    return lax.rsqrt(jnp.mean(x32 * x32, axis=-1, keepdims=True) + eps)


def _layer_spec(l, shape, col=None):
    if col is None:
        return pl.BlockSpec((None,) + shape, lambda *g: (l, 0, 0))
    return pl.BlockSpec((None,) + shape, lambda *g: (l, 0, col(*g)))


def _rope_table_kernel(pos_ref, invf_ref, m1_ref, m2_ref, c_ref, s1_ref, s2_ref):
    ang = pos_ref[...].astype(jnp.float32) * invf_ref[...]
    sn = jnp.sin(ang)
    c_ref[...] = jnp.cos(ang)
    s1_ref[...] = sn * m1_ref[...]
    s2_ref[...] = sn * m2_ref[...]


def _rope_tables(pos_col, invf, m1, m2):
    m = pos_col.shape[0]
    row = pl.BlockSpec((1, LANES), lambda i: (0, 0))
    tab = pl.BlockSpec((TM_PREP, LANES), lambda i: (i, 0))
    return pl.pallas_call(
        _rope_table_kernel,
        grid=(m // TM_PREP,),
        in_specs=[pl.BlockSpec((TM_PREP, 1), lambda i: (i, 0)), row, row, row],
        out_specs=[tab, tab, tab],
        out_shape=[jax.ShapeDtypeStruct((m, LANES), jnp.float32)] * 3,
        compiler_params=_cparams(("parallel",)),
        name="rope_tables",
    )(pos_col, invf, m1, m2)


def _rope_lane_patterns():
    lane = np.arange(LANES)
    half_a = PARTIAL_ROT_DIM // 2
    r = lane % DIFF_QK_DIM
    inv_a = ROPE_THETA ** (-jnp.arange(0, PARTIAL_ROT_DIM, 2, dtype=jnp.float32) / PARTIAL_ROT_DIM)
    invf_a = jnp.where(jnp.asarray(r < PARTIAL_ROT_DIM), inv_a[jnp.asarray(r % half_a)], 0.0)
    m1_a = np.where(r < half_a, -1.0, 0.0)
    m2_a = np.where((r >= half_a) & (r < PARTIAL_ROT_DIM), 1.0, 0.0)
    half_b = MLA_QK_ROPE // 2
    inv_b = ROPE_THETA ** (-jnp.arange(0, MLA_QK_ROPE, 2, dtype=jnp.float32) / MLA_QK_ROPE)
    invf_b = jnp.where(jnp.asarray(lane < MLA_QK_ROPE), inv_b[jnp.asarray(lane % half_b)], 0.0)
    m1_b = np.where(lane < half_b, -1.0, 0.0)
    m2_b = np.where((lane >= half_b) & (lane < MLA_QK_ROPE), 1.0, 0.0)
    f = lambda a: jnp.asarray(a, jnp.float32).reshape(1, LANES)
    return (f(invf_a), f(m1_a), f(m2_a)), (f(invf_b), f(m1_b), f(m2_b))


def _rope_block(x, c, s1, s2, d):
    return x * c + pltpu.roll(x, LANES - d, 1) * s1 + pltpu.roll(x, d, 1) * s2


def _in_proj_kernel(x_ref, g_ref, w_ref, z_ref, zg_ref, h_ref):
    j = pl.program_id(1)

    @pl.when(j == 0)
    def _():
        x = x_ref[...]
        h_ref[...] = (x * _rms_scale(x, NORM_EPS) * g_ref[...]).astype(jnp.bfloat16)

    acc = jnp.dot(h_ref[...], w_ref[...], preferred_element_type=jnp.float32)
    z_ref[...] = acc.astype(jnp.bfloat16)

    @pl.when(j == Z_FG // TN_IN)
    def _():
        off = Z_FG % TN_IN // LANES * LANES
        zg_ref[...] = acc[:, off:off + LANES]


def _in_proj(l, x, gain, w):
    m, d = x.shape
    return pl.pallas_call(
        _in_proj_kernel,
        grid=(m // TM_PROJ, Z_WIDTH // TN_IN),
        in_specs=[pl.BlockSpec((TM_PROJ, d), lambda i, j: (i, 0)),
                  _layer_spec(l, (1, d)),
                  _layer_spec(l, (d, TN_IN), lambda i, j: j)],
        out_specs=[pl.BlockSpec((TM_PROJ, TN_IN), lambda i, j: (i, j)),
                   pl.BlockSpec((TM_PROJ, LANES), lambda i, j: (i, 0))],
        out_shape=[jax.ShapeDtypeStruct((m, Z_WIDTH), jnp.bfloat16),
                   jax.ShapeDtypeStruct((m, LANES), jnp.float32)],
        scratch_shapes=[pltpu.VMEM((TM_PROJ, d), jnp.bfloat16)],
        compiler_params=_cparams(("parallel", "arbitrary")),
        name="in_proj",
    )(x, gain, w)


def _diff_rope_kernel(q_ref, k_ref, c_ref, s1_ref, s2_ref, o_ref):
    c, s1, s2 = c_ref[...], s1_ref[...], s2_ref[...]
    half = PARTIAL_ROT_DIM // 2
    scale = DIFF_QK_DIM ** -0.5
    nblk = q_ref.shape[1] // LANES
    for b in range(nblk):
        sl = slice(b * LANES, (b + 1) * LANES)
        q = _rope_block(q_ref[:, sl].astype(jnp.float32), c, s1, s2, half)
        o_ref[:, sl] = (q * scale).astype(jnp.bfloat16)
        k = _rope_block(k_ref[:, sl].astype(jnp.float32), c, s1, s2, half)
        o_ref[:, nblk * LANES + b * LANES: nblk * LANES + (b + 1) * LANES] = k.astype(jnp.bfloat16)


def _diff_rope(z, tabs):
    m = z.shape[0]
    w = DIFF_HEADS * 2 * DIFF_QK_DIM
    tab = pl.BlockSpec((TM_PREP, LANES), lambda i: (i, 0))
    return pl.pallas_call(
        _diff_rope_kernel,
        grid=(m // TM_PREP,),
        in_specs=[pl.BlockSpec((TM_PREP, w), lambda i: (i, Z_AQ // w)),
                  pl.BlockSpec((TM_PREP, w), lambda i: (i, Z_AK // w)),
                  tab, tab, tab],
        out_specs=pl.BlockSpec((TM_PREP, 2 * w), lambda i: (i, 0)),
        out_shape=jax.ShapeDtypeStruct((m, 2 * w), jnp.bfloat16),
        compiler_params=_cparams(("parallel",)),
        name="diff_rope",
    )(z, z, *tabs)


def _mla_up_kernel(cq_ref, ckv_ref, kr_ref, gq_ref, gkv_ref, wq_ref, wk_ref, wv_ref,
                   c_ref, s1_ref, s2_ref, q_ref, k_ref, v_ref):
    c, s1, s2 = c_ref[...], s1_ref[...], s2_ref[...]
    half = MLA_QK_ROPE // 2
    scale = (MLA_QK_NOPE + MLA_QK_ROPE) ** -0.5

    cq = cq_ref[...].astype(jnp.float32)
    cqn = (cq * _rms_scale(cq, NORM_EPS) * gq_ref[...]).astype(jnp.bfloat16)
    q = jnp.dot(cqn, wq_ref[...], preferred_element_type=jnp.float32)
    ckv = ckv_ref[...].astype(jnp.float32)
    ckvn = (ckv * _rms_scale(ckv, NORM_EPS) * gkv_ref[...]).astype(jnp.bfloat16)
    kn = jnp.dot(ckvn, wk_ref[...], preferred_element_type=jnp.float32)
    v_ref[...] = jnp.dot(ckvn, wv_ref[...], preferred_element_type=jnp.float32).astype(jnp.bfloat16)
    kr = _rope_block(kr_ref[...].astype(jnp.float32), c, s1, s2, half)
    lane = lax.broadcasted_iota(jnp.int32, kr.shape, 1)
    kr = jnp.where(lane < MLA_QK_ROPE, kr, 0.0).astype(jnp.bfloat16)

    for h in range(MLA_HEADS):
        base = h * MLA_QK_PAD
        q_ref[:, base:base + LANES] = (q[:, base:base + LANES] * scale).astype(jnp.bfloat16)
        qr = _rope_block(q[:, base + LANES:base + 2 * LANES], c, s1, s2, half)
        q_ref[:, base + LANES:base + 2 * LANES] = (qr * scale).astype(jnp.bfloat16)
        k_ref[:, base:base + LANES] = kn[:, h * LANES:(h + 1) * LANES].astype(jnp.bfloat16)
        k_ref[:, base + LANES:base + 2 * LANES] = kr


def _mla_up(l, z, gq, gkv, wq, wk, wv, tabs):
    m = z.shape[0]
    hq = MLA_HEADS * MLA_QK_PAD
    hv = MLA_HEADS * HEAD_DIM
    tab = pl.BlockSpec((TM_PREP, LANES), lambda i: (i, 0))
    return pl.pallas_call(
        _mla_up_kernel,
        grid=(m // TM_PREP,),
        in_specs=[pl.BlockSpec((TM_PREP, MLA_LORA), lambda i: (i, Z_CQ // MLA_LORA)),
                  pl.BlockSpec((TM_PREP, MLA_LORA), lambda i: (i, Z_CKV // MLA_LORA)),
                  pl.BlockSpec((TM_PREP, LANES), lambda i: (i, Z_KR // LANES)),
                  _layer_spec(l, (1, MLA_LORA)), _layer_spec(l, (1, MLA_LORA)),
                  _layer_spec(l, (MLA_LORA, hq)), _layer_spec(l, (MLA_LORA, hv)),
                  _layer_spec(l, (MLA_LORA, hv)),
                  tab, tab, tab],
        out_specs=[pl.BlockSpec((TM_PREP, hq), lambda i: (i, 0)),
                   pl.BlockSpec((TM_PREP, hq), lambda i: (i, 0)),
                   pl.BlockSpec((TM_PREP, hv), lambda i: (i, 0))],
        out_shape=[jax.ShapeDtypeStruct((m, hq), jnp.bfloat16),
                   jax.ShapeDtypeStruct((m, hq), jnp.bfloat16),
                   jax.ShapeDtypeStruct((m, hv), jnp.bfloat16)],
        compiler_params=_cparams(("parallel",)),
        name="mla_up",
    )(z, z, z, gq, gkv, wq, wk, wv, *tabs)


def _fox_gate_kernel(zg_ref, fb_ref, col_ref, row_ref):
    x = zg_ref[...] + fb_ref[...]
    logf = jnp.minimum(x, 0.0) - jnp.log(1.0 + jnp.exp(-jnp.abs(x)))
    s = logf.shape[0]
    row = lax.broadcasted_iota(jnp.int32, logf.shape, 0)
    d = 1
    while d < s:
        logf = logf + jnp.where(row >= d, pltpu.roll(logf, d, 0), 0.0)
        d *= 2
    col_ref[...] = logf
    row_ref[0] = logf.T[FOX_SHIFT:FOX_SHIFT + SUBLANES, :]


def _fox_gate(l, zg, fb, batch, seq):
    return pl.pallas_call(
        _fox_gate_kernel,
        grid=(batch,),
        in_specs=[pl.BlockSpec((seq, LANES), lambda b: (b, 0)),
                  _layer_spec(l, (1, LANES))],
        out_specs=[pl.BlockSpec((seq, LANES), lambda b: (b, 0)),
                   pl.BlockSpec((1, SUBLANES, seq), lambda b: (b, 0, 0))],
        out_shape=[jax.ShapeDtypeStruct((batch * seq, LANES), jnp.float32),
                   jax.ShapeDtypeStruct((batch, SUBLANES, seq), jnp.float32)],
        compiler_params=_cparams(("parallel",)),
        name="fox_gate",
    )(zg, fb)


_NT = (((1,), (1,)), ((), ()))


def _softmax_tile(q, k_ref, s_ref, p_ref, i, bias_fn=None):
    tq = q.shape[0]
    mpart = jnp.full((tq, LANES), -jnp.inf, jnp.float32)
    for c in range(i + 1):
        cs = slice(c * TQ, (c + 1) * TQ)
        s = lax.dot_general(q, k_ref[cs, :], _NT, preferred_element_type=jnp.float32)
        if bias_fn is not None:
            s = s + bias_fn(cs)
        if c == i:
            rows = lax.broadcasted_iota(jnp.int32, s.shape, 0)
            cols = lax.broadcasted_iota(jnp.int32, s.shape, 1)
            s = jnp.where(cols <= rows, s, -jnp.inf)
        s_ref[:, cs] = s
        for b in range(TQ // LANES):
            mpart = jnp.maximum(mpart, s[:, b * LANES:(b + 1) * LANES])
    m = jnp.max(mpart, axis=1, keepdims=True)
    lpart = jnp.zeros((tq, LANES), jnp.float32)
    for c in range(i + 1):
        cs = slice(c * TQ, (c + 1) * TQ)
        p = jnp.exp(s_ref[:, cs] - m)
        for b in range(TQ // LANES):
            lpart = lpart + p[:, b * LANES:(b + 1) * LANES]
        p_ref[:, cs] = p.astype(jnp.bfloat16)
    return jnp.sum(lpart, axis=1, keepdims=True)


def _pv(p_ref, v_ref, i):
    n = (i + 1) * TQ
    return jnp.dot(p_ref[:, :n], v_ref[:n, :], preferred_element_type=jnp.float32)


def _mla_attn_kernel(q_ref, k_ref, v_ref, o_ref, s_ref, p_ref):
    for i in range(q_ref.shape[0] // TQ):
        rs = slice(i * TQ, (i + 1) * TQ)
        l = _softmax_tile(q_ref[rs, :], k_ref, s_ref, p_ref, i)
        o_ref[rs, :] = (_pv(p_ref, v_ref, i) / l).astype(o_ref.dtype)


def _unshift(lo, hi):
    return jnp.concatenate([lo[:, FOX_SHIFT:], hi[:, :FOX_SHIFT]], axis=1)


def _fox_attn_kernel(qlo_ref, qhi_ref, klo_ref, khi_ref, vlo_ref, vhi_ref, ccol_ref, crow_ref,
                     o_ref, s_ref, p_ref, k_ref, v_ref):
    h = pl.program_id(1)
    scale = HEAD_DIM ** -0.5
    k_ref[...] = _unshift(klo_ref[...], khi_ref[...])
    v_ref[...] = _unshift(vlo_ref[...], vhi_ref[...])
    crow = crow_ref[0, pl.ds(h, 1), :]
    for i in range(o_ref.shape[0] // TQ):
        rs = slice(i * TQ, (i + 1) * TQ)
        q = _unshift(qlo_ref[rs, :], qhi_ref[rs, :])
        q = (q.astype(jnp.float32) * scale).astype(jnp.bfloat16)
        cc = ccol_ref[rs, :]
        lane = lax.broadcasted_iota(jnp.int32, cc.shape, 1)
        cq = jnp.sum(jnp.where(lane == h + FOX_SHIFT, cc, 0.0), axis=1, keepdims=True)
        l = _softmax_tile(q, k_ref, s_ref, p_ref, i, bias_fn=lambda cs: cq - crow[:, cs])
        o_ref[rs, :] = (_pv(p_ref, v_ref, i) / l).astype(o_ref.dtype)


def _diff_attn_kernel(lambda_init, q_ref, k_ref, v_ref, lam_ref, g_ref, o_ref, s_ref, p_ref):
    lv = lam_ref[...]
    lam = (jnp.exp(jnp.sum(lv[0:1] * lv[1:2], axis=1, keepdims=True))
           - jnp.exp(jnp.sum(lv[2:3] * lv[3:4], axis=1, keepdims=True)) + lambda_init)
    gain = g_ref[...] * (1.0 - lambda_init)
    for i in range(q_ref.shape[0] // TQ):
        rs = slice(i * TQ, (i + 1) * TQ)
        q = q_ref[rs, :]
        lane = lax.broadcasted_iota(jnp.int32, q.shape, 1)
        zero = jnp.zeros_like(q)
        q1 = jnp.where(lane < DIFF_QK_DIM, q, zero)
        q2 = jnp.where(lane >= DIFF_QK_DIM, q, zero)
        l1 = _softmax_tile(q1, k_ref, s_ref, p_ref, i)
        o1 = _pv(p_ref, v_ref, i) / l1
        l2 = _softmax_tile(q2, k_ref, s_ref, p_ref, i)
        o2 = _pv(p_ref, v_ref, i) / l2
        o = o1 - lam * o2
        o_ref[rs, :] = (o * _rms_scale(o, SUBLN_EPS) * gain).astype(o_ref.dtype)


def _attn_call(kernel, name, batch, seq, heads, ins, in_specs, extra_scratch=()):
    return pl.pallas_call(
        kernel,
        grid=(batch, heads),
        in_specs=in_specs,
        out_specs=pl.BlockSpec((seq, HEAD_DIM), lambda b, h: (b, h)),
        out_shape=jax.ShapeDtypeStruct((batch * seq, heads * HEAD_DIM), jnp.bfloat16),
        scratch_shapes=[pltpu.VMEM((TQ, seq), jnp.float32), pltpu.VMEM((TQ, seq), jnp.bfloat16),
                        *extra_scratch],
        compiler_params=_cparams(("parallel", "parallel")),
        name=name,
    )(*ins)


def _colblk(seq, width, col0):
    assert col0 % width == 0
    return pl.BlockSpec((seq, width), lambda b, h: (b, col0 // width + h))


def _diff_attn(l, qk, z, lam, gain, lambda_init, batch, seq):
    kern = functools.partial(_diff_attn_kernel, lambda_init)
    specs = [_colblk(seq, HEAD_DIM, 0), _colblk(seq, HEAD_DIM, DIFF_HEADS * HEAD_DIM),
             _colblk(seq, HEAD_DIM, Z_AV),
             _layer_spec(l, (4, DIFF_QK_DIM)), _layer_spec(l, (1, HEAD_DIM))]
    return _attn_call(kern, "diff_attn", batch, seq, DIFF_HEADS, (qk, qk, z, lam, gain), specs)


def _mla_attn(q, k, v, batch, seq):
    specs = [_colblk(seq, MLA_QK_PAD, 0), _colblk(seq, MLA_QK_PAD, 0), _colblk(seq, HEAD_DIM, 0)]
    return _attn_call(_mla_attn_kernel, "mla_attn", batch, seq, MLA_HEADS, (q, k, v), specs)


def _fox_attn(z, ccol, crow, batch, seq):
    specs = []
    for col0 in (Z_FQ, Z_FK, Z_FV):
        specs += [_colblk(seq, LANES, col0 - FOX_SHIFT), _colblk(seq, LANES, col0 - FOX_SHIFT + LANES)]
    specs += [pl.BlockSpec((seq, LANES), lambda b, h: (b, 0)),
              pl.BlockSpec((1, SUBLANES, seq), lambda b, h: (b, 0, 0))]
    kv = pltpu.VMEM((seq, HEAD_DIM), jnp.bfloat16)
    return _attn_call(_fox_attn_kernel, "fox_attn", batch, seq, FOX_HEADS, (z,) * 6 + (ccol, crow), specs,
                      extra_scratch=(kv, kv))


def _out_proj_kernel(x_ref, a_ref, b_ref, c_ref, wa_ref, wb_ref, wc_ref, o_ref):
    acc = jnp.dot(a_ref[...], wa_ref[...], preferred_element_type=jnp.float32)
    acc = acc + jnp.dot(b_ref[...], wb_ref[...], preferred_element_type=jnp.float32)
    acc = acc + jnp.dot(c_ref[...], wc_ref[...], preferred_element_type=jnp.float32)
    o_ref[...] = x_ref[...] + acc


def _out_proj(l, x, oa, ob, oc, wa, wb, wc):
    m, d = x.shape
    lhs = lambda a: pl.BlockSpec((TM_PROJ, a.shape[1]), lambda i, j: (i, 0))
    rhs = lambda w: _layer_spec(l, (w.shape[1], TN_OUT), lambda i, j: j)
    xo = pl.BlockSpec((TM_PROJ, TN_OUT), lambda i, j: (i, j))
    return pl.pallas_call(
        _out_proj_kernel,
        grid=(m // TM_PROJ, d // TN_OUT),
        in_specs=[xo, lhs(oa), lhs(ob), lhs(oc), rhs(wa), rhs(wb), rhs(wc)],
        out_specs=xo,
        out_shape=jax.ShapeDtypeStruct((m, d), jnp.float32),
        compiler_params=_cparams(("parallel", "parallel")),
        name="out_proj",
    )(x, oa, ob, oc, wa, wb, wc)


def _ffn_up_kernel(tiles_per_seq, x_ref, g_ref, wa_ref, wg_ref, cwa_ref, cwg_ref, cba_ref, cbg_ref,
                   o_ref, h_ref, ua_ref, ug_ref, carry_ref):
    i, j = pl.program_id(0), pl.program_id(1)
    tm = x_ref.shape[0]

    @pl.when(j == 0)
    def _():
        x = x_ref[...]
        h_ref[...] = (x * _rms_scale(x, NORM_EPS) * g_ref[...]).astype(jnp.bfloat16)

    @pl.when(i % tiles_per_seq == 0)
    def _():
        carry_ref[j] = jnp.zeros(carry_ref.shape[1:], jnp.float32)

    def conv(u_ref, w_ref, half, cw_ref, cb_ref):
        u_ref[0:SUBLANES, :] = carry_ref[j, half]
        u = jnp.dot(h_ref[...], w_ref[...], preferred_element_type=jnp.float32)
        u_ref[SUBLANES:, :] = u
        carry_ref[j, half] = u[tm - SUBLANES:, :]
        cw = cw_ref[...]
        return (u * cw[2:3] + u_ref[SUBLANES - 1:SUBLANES - 1 + tm, :] * cw[1:2]
                + u_ref[SUBLANES - 2:SUBLANES - 2 + tm, :] * cw[0:1] + cb_ref[...])

    a = conv(ua_ref, wa_ref, 0, cwa_ref, cba_ref)
    g = conv(ug_ref, wg_ref, 1, cwg_ref, cbg_ref)
    o_ref[...] = (g / (1.0 + jnp.exp(-g)) * a).astype(o_ref.dtype)


def _ffn_up(l, x, gain, w_up, conv_w, conv_b, seq):
    m, d = x.shape
    nj = D_FF // TN_FF
    kern = functools.partial(_ffn_up_kernel, seq // TM_PROJ)
    col = lambda rows, off: _layer_spec(l, (rows, TN_FF), lambda i, j: j + off)
    return pl.pallas_call(
        kern,
        grid=(m // TM_PROJ, nj),
        in_specs=[pl.BlockSpec((TM_PROJ, d), lambda i, j: (i, 0)),
                  _layer_spec(l, (1, d)),
                  col(d, 0), col(d, nj), col(3, 0), col(3, nj), col(1, 0), col(1, nj)],
        out_specs=pl.BlockSpec((TM_PROJ, TN_FF), lambda i, j: (i, j)),
        out_shape=jax.ShapeDtypeStruct((m, D_FF), jnp.bfloat16),
        scratch_shapes=[pltpu.VMEM((TM_PROJ, d), jnp.bfloat16),
                        pltpu.VMEM((TM_PROJ + SUBLANES, TN_FF), jnp.float32),
                        pltpu.VMEM((TM_PROJ + SUBLANES, TN_FF), jnp.float32),
                        pltpu.VMEM((nj, 2, SUBLANES, TN_FF), jnp.float32)],
        compiler_params=_cparams(("arbitrary", "arbitrary")),
        name="ffn_up",
    )(x, gain, w_up, w_up, conv_w, conv_w, conv_b, conv_b)


def _ffn_down_kernel(x_ref, a_ref, w_ref, o_ref):
    o_ref[...] = x_ref[...] + jnp.dot(a_ref[...], w_ref[...], preferred_element_type=jnp.float32)


def _ffn_down(l, x, act, w):
    m, d = x.shape
    k = act.shape[1]
    xo = pl.BlockSpec((TM_PROJ, TN_DOWN), lambda i, j: (i, j))
    return pl.pallas_call(
        _ffn_down_kernel,
        grid=(m // TM_PROJ, d // TN_DOWN),
        in_specs=[xo, pl.BlockSpec((TM_PROJ, k), lambda i, j: (i, 0)),
                  _layer_spec(l, (k, TN_DOWN), lambda i, j: j)],
        out_specs=xo,
        out_shape=jax.ShapeDtypeStruct((m, d), jnp.float32),
        compiler_params=_cparams(("parallel", "parallel")),
        name="ffn_down",
    )(x, act, w)


def _final_norm_kernel(x_ref, g_ref, o_ref):
    x = x_ref[...]
    o_ref[...] = x * _rms_scale(x, NORM_EPS) * g_ref[...]


def _final_norm(x, gain):
    m, d = x.shape
    blk = pl.BlockSpec((TM_PREP, d), lambda i: (i, 0))
    return pl.pallas_call(
        _final_norm_kernel,
        grid=(m // TM_PREP,),
        in_specs=[blk, pl.BlockSpec((1, d), lambda i: (0, 0))],
        out_specs=blk,
        out_shape=jax.ShapeDtypeStruct((m, d), jnp.float32),
        compiler_params=_cparams(("parallel",)),
        name="final_norm",
    )(x, gain)


def _cast_pad_kernel(w_ref, o_ref):
    n = w_ref.shape[1]
    full = n // LANES * LANES
    o_ref[:, :full] = w_ref[:, :full].astype(jnp.bfloat16)
    o_ref[:, full:] = jnp.zeros((o_ref.shape[0], o_ref.shape[1] - full), jnp.bfloat16)
    o_ref[:, full:n] = w_ref[:, full:].astype(jnp.bfloat16)


def _cast_pad(w, width, rows_per_step=256):
    dp, k, n = w.shape
    return pl.pallas_call(
        _cast_pad_kernel,
        grid=(dp, k // rows_per_step),
        in_specs=[pl.BlockSpec((None, rows_per_step, n), lambda l, r: (l, r, 0))],
        out_specs=pl.BlockSpec((None, rows_per_step, width), lambda l, r: (l, r, 0)),
        out_shape=jax.ShapeDtypeStruct((dp, k, width), jnp.bfloat16),
        compiler_params=_cparams(("parallel", "parallel")),
        name="cast_pad",
    )(w)


def _pad_w_uq(w):
    dp, r, _ = w.shape
    w = w.reshape(dp, r, MLA_HEADS, MLA_QK_NOPE + MLA_QK_ROPE)
    w = jnp.pad(w, ((0, 0), (0, 0), (0, 0), (0, MLA_QK_PAD - MLA_QK_NOPE - MLA_QK_ROPE)))
    return w.reshape(dp, r, MLA_HEADS * MLA_QK_PAD).astype(jnp.bfloat16)


def _split_w_ukv(w):
    dp, r, _ = w.shape
    w = w.reshape(dp, r, MLA_HEADS, MLA_QK_NOPE + HEAD_DIM)
    wk = w[..., :MLA_QK_NOPE].reshape(dp, r, MLA_HEADS * MLA_QK_NOPE)
    wv = w[..., MLA_QK_NOPE:].reshape(dp, r, MLA_HEADS * HEAD_DIM)
    return wk.astype(jnp.bfloat16), wv.astype(jnp.bfloat16)


def kernel(x, positions, attn_norm, w_in, diff_lambda, diff_out_norm, mla_q_norm, mla_kv_norm, mla_w_uq,
           mla_w_ukv, fox_forget_bias, w_o, ffn_norm, ffn_w_up, ffn_conv_w, ffn_conv_b, ffn_w_down,
           final_norm):
    batch, seq, d = x.shape
    depth = w_in.shape[0]
    m = batch * seq
    assert seq % TQ == 0 and seq % TM_PROJ == 0 and m % TM_PROJ == 0
    assert w_in.shape[2] == IN_WIDTH

    pat_a, pat_b = _rope_lane_patterns()
    pos_col = positions.reshape(m, 1)
    tabs_a = _rope_tables(pos_col, *pat_a)
    tabs_b = _rope_tables(pos_col, *pat_b)

    bf = lambda a: a.astype(jnp.bfloat16)
    rows = lambda a: a.reshape(depth, 1, -1)
    n_a, n_b = DIFF_HEADS * HEAD_DIM, MLA_HEADS * HEAD_DIM
    w_in_b = _cast_pad(w_in, Z_WIDTH)
    w_uq_b = _pad_w_uq(mla_w_uq)
    w_uk_b, w_uv_b = _split_w_ukv(mla_w_ukv)
    wo_a, wo_b, wo_c = bf(w_o[:, :n_a]), bf(w_o[:, n_a:n_a + n_b]), bf(w_o[:, n_a + n_b:])
    w_up_b, w_down_b = bf(ffn_w_up), bf(ffn_w_down)
    fb = jnp.pad(fox_forget_bias, ((0, 0), (FOX_SHIFT, LANES - FOX_SHIFT - FOX_HEADS))).reshape(depth, 1, LANES)
    g_attn, g_ffn, g_q, g_kv, g_diff = (rows(attn_norm), rows(ffn_norm), rows(mla_q_norm),
                                        rows(mla_kv_norm), rows(diff_out_norm))
    conv_b = rows(ffn_conv_b)

    xf = x.reshape(m, d)
    for l in range(depth):
        lambda_init = 0.8 - 0.6 * math.exp(-0.3 * l)
        z, zg = _in_proj(l, xf, g_attn, w_in_b)
        qk_a = _diff_rope(z, tabs_a)
        q_b, k_b, v_b = _mla_up(l, z, g_q, g_kv, w_uq_b, w_uk_b, w_uv_b, tabs_b)
        ccol, crow = _fox_gate(l, zg, fb, batch, seq)
        o_a = _diff_attn(l, qk_a, z, diff_lambda, g_diff, lambda_init, batch, seq)
        o_b = _mla_attn(q_b, k_b, v_b, batch, seq)
        o_c = _fox_attn(z, ccol, crow, batch, seq)
        xf = _out_proj(l, xf, o_a, o_b, o_c, wo_a, wo_b, wo_c)
        act = _ffn_up(l, xf, g_ffn, w_up_b, ffn_conv_w, conv_b, seq)
        xf = _ffn_down(l, xf, act, w_down_b)
    return _final_norm(xf, final_norm.reshape(1, d)).reshape(batch, seq, d)
```

```python
import functools
import math

import numpy as np
import jax
import jax.numpy as jnp
from jax import lax
from jax.experimental import pallas as pl
from jax.experimental.pallas import tpu as pltpu

HEAD_DIM = 128
DIFF_HEADS = 4
DIFF_QK_DIM = 64
MLA_HEADS = 6
MLA_LORA = 512
MLA_QK_NOPE = 128
MLA_QK_ROPE = 64
FOX_HEADS = 6
D_FF = 5632
ROPE_THETA = 500000.0
PARTIAL_ROT_DIM = DIFF_QK_DIM // 4
NORM_EPS = 1e-6
SUBLN_EPS = 1e-5
LOG2E = math.log2(math.e)

LANES = 128
SUBLANES = 8
VMEM_LIMIT_BYTES = 56 * 1024 * 1024

Z_AQ, Z_AK, Z_AV = 0, 512, 1024
Z_CQ, Z_CKV, Z_KR = 1536, 2048, 2560
Z_FQ, Z_FK, Z_FV = 2624, 3392, 4160
Z_FG = 4928
IN_WIDTH = Z_FG + FOX_HEADS
FOX_SHIFT = Z_FQ % LANES
assert Z_FK % LANES == FOX_SHIFT and Z_FV % LANES == FOX_SHIFT and Z_FG % LANES == FOX_SHIFT
MLA_QK_PAD = 256

TM_PROJ = 1024
TN_IN = 1024
TN_OUT = 1024
TN_FF = 512
TN_DOWN = 512
TM_PREP = 512
TQ = 256
Z_WIDTH = -(-IN_WIDTH // TN_IN) * TN_IN


def _cparams(sem):
    return pltpu.CompilerParams(dimension_semantics=sem, vmem_limit_bytes=VMEM_LIMIT_BYTES)


def _rms_scale(x32, eps):
    return lax.rsqrt(jnp.mean(x32 * x32, axis=-1, keepdims=True) + eps)


def _layer_spec(l, shape, col=None):
    if col is None:
        return pl.BlockSpec((None,) + shape, lambda *g: (l, 0, 0))
    return pl.BlockSpec((None,) + shape, lambda *g: (l, 0, col(*g)))


def _rope_table_kernel(pos_ref, invf_ref, m1_ref, m2_ref, c_ref, s1_ref, s2_ref):
    ang = pos_ref[...].astype(jnp.float32) * invf_ref[...]
    sn = jnp.sin(ang)
    c_ref[...] = jnp.cos(ang)
    s1_ref[...] = sn * m1_ref[...]
    s2_ref[...] = sn * m2_ref[...]


def _rope_tables(pos_col, invf, m1, m2):
    m = pos_col.shape[0]
    row = pl.BlockSpec((1, LANES), lambda i: (0, 0))
    tab = pl.BlockSpec((TM_PREP, LANES), lambda i: (i, 0))
    return pl.pallas_call(
        _rope_table_kernel,
        grid=(m // TM_PREP,),
        in_specs=[pl.BlockSpec((TM_PREP, 1), lambda i: (i, 0)), row, row, row],
        out_specs=[tab, tab, tab],
        out_shape=[jax.ShapeDtypeStruct((m, LANES), jnp.float32)] * 3,
        compiler_params=_cparams(("parallel",)),
        name="rope_tables",
    )(pos_col, invf, m1, m2)


def _rope_lane_patterns():
    lane = np.arange(LANES)
    half_a = PARTIAL_ROT_DIM // 2
    r = lane % DIFF_QK_DIM
    inv_a = ROPE_THETA ** (-jnp.arange(0, PARTIAL_ROT_DIM, 2, dtype=jnp.float32) / PARTIAL_ROT_DIM)
    invf_a = jnp.where(jnp.asarray(r < PARTIAL_ROT_DIM), inv_a[jnp.asarray(r % half_a)], 0.0)
    m1_a = np.where(r < half_a, -1.0, 0.0)
    m2_a = np.where((r >= half_a) & (r < PARTIAL_ROT_DIM), 1.0, 0.0)
    half_b = MLA_QK_ROPE // 2
    inv_b = ROPE_THETA ** (-jnp.arange(0, MLA_QK_ROPE, 2, dtype=jnp.float32) / MLA_QK_ROPE)
    invf_b = jnp.where(jnp.asarray(lane < MLA_QK_ROPE), inv_b[jnp.asarray(lane % half_b)], 0.0)
    m1_b = np.where(lane < half_b, -1.0, 0.0)
    m2_b = np.where((lane >= half_b) & (lane < MLA_QK_ROPE), 1.0, 0.0)
    f = lambda a: jnp.asarray(a, jnp.float32).reshape(1, LANES)
    return (f(invf_a), f(m1_a), f(m2_a)), (f(invf_b), f(m1_b), f(m2_b))


def _rope_block(x, c, s1, s2, d):
    return x * c + pltpu.roll(x, LANES - d, 1) * s1 + pltpu.roll(x, d, 1) * s2


def _in_proj_kernel(x_ref, g_ref, w_ref, c_ref, s1_ref, s2_ref, z_ref, zg_ref, h_ref):
    j = pl.program_id(1)

    @pl.when(j == 0)
    def _():
        x = x_ref[...]
        h_ref[...] = (x * _rms_scale(x, NORM_EPS) * g_ref[...]).astype(jnp.bfloat16)

    acc = lax.dot_general(h_ref[...], w_ref[...], (((1,), (1,)), ((), ())),
                          preferred_element_type=jnp.float32)

    @pl.when(j == 0)
    def _():
        c, s1, s2 = c_ref[...], s1_ref[...], s2_ref[...]
        scale = DIFF_QK_DIM ** -0.5 * LOG2E
        for b in range(TN_IN // LANES):
            sl = slice(b * LANES, (b + 1) * LANES)
            r = _rope_block(acc[:, sl], c, s1, s2, PARTIAL_ROT_DIM // 2)
            z_ref[:, sl] = ((r * scale) if b * LANES < Z_AK else r).astype(jnp.bfloat16)

    @pl.when(j != 0)
    def _():
        z_ref[...] = acc.astype(jnp.bfloat16)

    @pl.when(j == Z_FG // TN_IN)
    def _():
        off = Z_FG % TN_IN // LANES * LANES
        zg_ref[...] = acc[:, off:off + LANES]


def _in_proj(l, x, gain, w, tabs):
    m, d = x.shape
    assert Z_AQ == 0 and TN_IN == Z_AV
    tab = pl.BlockSpec((TM_PROJ, LANES), lambda i, j: (i, 0))
    return pl.pallas_call(
        _in_proj_kernel,
        grid=(m // TM_PROJ, Z_WIDTH // TN_IN),
        in_specs=[pl.BlockSpec((TM_PROJ, d), lambda i, j: (i, 0)),
                  _layer_spec(l, (1, d)),
                  pl.BlockSpec((None, TN_IN, d), lambda i, j: (l, j, 0)),
                  tab, tab, tab],
        out_specs=[pl.BlockSpec((TM_PROJ, TN_IN), lambda i, j: (i, j)),
                   pl.BlockSpec((TM_PROJ, LANES), lambda i, j: (i, 0))],
        out_shape=[jax.ShapeDtypeStruct((m, Z_WIDTH), jnp.bfloat16),
                   jax.ShapeDtypeStruct((m, LANES), jnp.float32)],
        scratch_shapes=[pltpu.VMEM((TM_PROJ, d), jnp.bfloat16)],
        compiler_params=_cparams(("parallel", "arbitrary")),
        name="in_proj",
    )(x, gain, w, *tabs)


def _mla_up_kernel(cq_ref, ckv_ref, kr_ref, gq_ref, gkv_ref, wq_ref, wk_ref, wv_ref,
                   c_ref, s1_ref, s2_ref, q_ref, k_ref, v_ref):
    c, s1, s2 = c_ref[...], s1_ref[...], s2_ref[...]
    half = MLA_QK_ROPE // 2
    scale = (MLA_QK_NOPE + MLA_QK_ROPE) ** -0.5 * LOG2E

    cq = cq_ref[...].astype(jnp.float32)
    cqn = (cq * _rms_scale(cq, NORM_EPS) * gq_ref[...]).astype(jnp.bfloat16)
    q = jnp.dot(cqn, wq_ref[...], preferred_element_type=jnp.float32)
    ckv = ckv_ref[...].astype(jnp.float32)
    ckvn = (ckv * _rms_scale(ckv, NORM_EPS) * gkv_ref[...]).astype(jnp.bfloat16)
    kn = jnp.dot(ckvn, wk_ref[...], preferred_element_type=jnp.float32)
    v_ref[...] = jnp.dot(ckvn, wv_ref[...], preferred_element_type=jnp.float32).astype(jnp.bfloat16)
    kr = _rope_block(kr_ref[...].astype(jnp.float32), c, s1, s2, half)
    lane = lax.broadcasted_iota(jnp.int32, kr.shape, 1)
    kr = jnp.where(lane < MLA_QK_ROPE, kr, 0.0).astype(jnp.bfloat16)

    for h in range(MLA_HEADS):
        base = h * MLA_QK_PAD
        q_ref[:, base:base + LANES] = (q[:, base:base + LANES] * scale).astype(jnp.bfloat16)
        qr = _rope_block(q[:, base + LANES:base + 2 * LANES], c, s1, s2, half)
        q_ref[:, base + LANES:base + 2 * LANES] = (qr * scale).astype(jnp.bfloat16)
        k_ref[:, base:base + LANES] = kn[:, h * LANES:(h + 1) * LANES].astype(jnp.bfloat16)
        k_ref[:, base + LANES:base + 2 * LANES] = kr


def _mla_up(l, z, gq, gkv, wq, wk, wv, tabs):
    m = z.shape[0]
    hq = MLA_HEADS * MLA_QK_PAD
    hv = MLA_HEADS * HEAD_DIM
    tab = pl.BlockSpec((TM_PREP, LANES), lambda i: (i, 0))
    return pl.pallas_call(
        _mla_up_kernel,
        grid=(m // TM_PREP,),
        in_specs=[pl.BlockSpec((TM_PREP, MLA_LORA), lambda i: (i, Z_CQ // MLA_LORA)),
                  pl.BlockSpec((TM_PREP, MLA_LORA), lambda i: (i, Z_CKV // MLA_LORA)),
                  pl.BlockSpec((TM_PREP, LANES), lambda i: (i, Z_KR // LANES)),
                  _layer_spec(l, (1, MLA_LORA)), _layer_spec(l, (1, MLA_LORA)),
                  _layer_spec(l, (MLA_LORA, hq)), _layer_spec(l, (MLA_LORA, hv)),
                  _layer_spec(l, (MLA_LORA, hv)),
                  tab, tab, tab],
        out_specs=[pl.BlockSpec((TM_PREP, hq), lambda i: (i, 0)),
                   pl.BlockSpec((TM_PREP, hq), lambda i: (i, 0)),
                   pl.BlockSpec((TM_PREP, hv), lambda i: (i, 0))],
        out_shape=[jax.ShapeDtypeStruct((m, hq), jnp.bfloat16),
                   jax.ShapeDtypeStruct((m, hq), jnp.bfloat16),
                   jax.ShapeDtypeStruct((m, hv), jnp.bfloat16)],
        compiler_params=_cparams(("parallel",)),
        name="mla_up",
    )(z, z, z, gq, gkv, wq, wk, wv, *tabs)


def _fox_gate_kernel(zg_ref, fb_ref, col_ref, row_ref):
    x = zg_ref[...] + fb_ref[...]
    logf = jnp.minimum(x, 0.0) - jnp.log(1.0 + jnp.exp(-jnp.abs(x)))
    s = logf.shape[0]
    row = lax.broadcasted_iota(jnp.int32, logf.shape, 0)
    d = 1
    while d < s:
        logf = logf + jnp.where(row >= d, pltpu.roll(logf, d, 0), 0.0)
        d *= 2
    logf = logf * LOG2E
    col_ref[...] = logf
    row_ref[0] = logf.T[FOX_SHIFT:FOX_SHIFT + SUBLANES, :]


def _fox_gate(l, zg, fb, batch, seq):
    return pl.pallas_call(
        _fox_gate_kernel,
        grid=(batch,),
        in_specs=[pl.BlockSpec((seq, LANES), lambda b: (b, 0)),
                  _layer_spec(l, (1, LANES))],
        out_specs=[pl.BlockSpec((seq, LANES), lambda b: (b, 0)),
                   pl.BlockSpec((1, SUBLANES, seq), lambda b: (b, 0, 0))],
        out_shape=[jax.ShapeDtypeStruct((batch * seq, LANES), jnp.float32),
                   jax.ShapeDtypeStruct((batch, SUBLANES, seq), jnp.float32)],
        compiler_params=_cparams(("parallel",)),
        name="fox_gate",
    )(zg, fb)


_NT = (((1,), (1,)), ((), ()))


def _scores(q, k_ref, s_ref, bias_fn=None):
    n = s_ref.shape[0] // TQ
    for c in range(n):
        cs = slice(c * TQ, (c + 1) * TQ)
        s = lax.dot_general(q[c * TQ:, :], k_ref[cs, :], _NT, preferred_element_type=jnp.float32)
        if bias_fn is not None:
            s = s + bias_fn(c)
        s_ref[c * TQ:, cs] = s


def _softmax_tile(s_ref, p_ref, i):
    rs = slice(i * TQ, (i + 1) * TQ)

    def chunk(c):
        s = s_ref[rs, c * TQ:(c + 1) * TQ]
        if c == i:
            rows = lax.broadcasted_iota(jnp.int32, s.shape, 0)
            cols = lax.broadcasted_iota(jnp.int32, s.shape, 1)
            s = jnp.where(cols <= rows, s, -jnp.inf)
        return s

    mpart = None
    for c in range(i + 1):
        s = chunk(c)
        for b in range(TQ // LANES):
            blk = s[:, b * LANES:(b + 1) * LANES]
            mpart = blk if mpart is None else jnp.maximum(mpart, blk)
    m = jnp.max(mpart, axis=1, keepdims=True)
    lpart = None
    for c in range(i + 1):
        p = jnp.exp2(chunk(c) - m)
        for b in range(TQ // LANES):
            blk = p[:, b * LANES:(b + 1) * LANES]
            lpart = blk if lpart is None else lpart + blk
        p_ref[rs, c * TQ:(c + 1) * TQ] = p.astype(jnp.bfloat16)
    return jnp.sum(lpart, axis=1, keepdims=True)


def _pv(p_ref, v_ref, i):
    n = (i + 1) * TQ
    return jnp.dot(p_ref[i * TQ:(i + 1) * TQ, :n], v_ref[:n, :], preferred_element_type=jnp.float32)


def _mla_attn_kernel(q_ref, k_ref, v_ref, o_ref, s_ref, p_ref):
    _scores(q_ref[...], k_ref, s_ref)
    for i in range(q_ref.shape[0] // TQ):
        l = _softmax_tile(s_ref, p_ref, i)
        o_ref[i * TQ:(i + 1) * TQ, :] = (_pv(p_ref, v_ref, i) / l).astype(o_ref.dtype)


def _unshift(lo, hi):
    return jnp.concatenate([lo[:, FOX_SHIFT:], hi[:, :FOX_SHIFT]], axis=1)


def _fox_attn_kernel(qlo_ref, qhi_ref, klo_ref, khi_ref, vlo_ref, vhi_ref, ccol_ref, crow_ref,
                     o_ref, s_ref, p_ref, k_ref, v_ref):
    h = pl.program_id(1)
    scale = HEAD_DIM ** -0.5 * LOG2E
    k_ref[...] = _unshift(klo_ref[...], khi_ref[...])
    v_ref[...] = _unshift(vlo_ref[...], vhi_ref[...])
    q = _unshift(qlo_ref[...], qhi_ref[...])
    q = (q.astype(jnp.float32) * scale).astype(jnp.bfloat16)
    crow = crow_ref[0, pl.ds(h, 1), :]
    cc = ccol_ref[...]
    lane = lax.broadcasted_iota(jnp.int32, cc.shape, 1)
    ccq = jnp.sum(jnp.where(lane == h + FOX_SHIFT, cc, 0.0), axis=1, keepdims=True)
    _scores(q, k_ref, s_ref, bias_fn=lambda c: ccq[c * TQ:, :] - crow[:, c * TQ:(c + 1) * TQ])
    for i in range(o_ref.shape[0] // TQ):
        l = _softmax_tile(s_ref, p_ref, i)
        o_ref[i * TQ:(i + 1) * TQ, :] = (_pv(p_ref, v_ref, i) / l).astype(o_ref.dtype)


def _diff_attn_kernel(lambda_init, q_ref, k_ref, v_ref, lam_ref, g_ref, o_ref, s_ref, p_ref, o1_ref):
    lv = lam_ref[...]
    lam = (jnp.exp(jnp.sum(lv[0:1] * lv[1:2], axis=1, keepdims=True))
           - jnp.exp(jnp.sum(lv[2:3] * lv[3:4], axis=1, keepdims=True)) + lambda_init)
    gain = g_ref[...] * (1.0 - lambda_init)
    q = q_ref[...]
    lane = lax.broadcasted_iota(jnp.int32, q.shape, 1)
    zero = jnp.zeros_like(q)
    nq = q_ref.shape[0] // TQ
    _scores(jnp.where(lane < DIFF_QK_DIM, q, zero), k_ref, s_ref)
    for i in range(nq):
        l = _softmax_tile(s_ref, p_ref, i)
        o1_ref[i * TQ:(i + 1) * TQ, :] = _pv(p_ref, v_ref, i) / l
    _scores(jnp.where(lane >= DIFF_QK_DIM, q, zero), k_ref, s_ref)
    for i in range(nq):
        rs = slice(i * TQ, (i + 1) * TQ)
        l = _softmax_tile(s_ref, p_ref, i)
        o = o1_ref[rs, :] - lam * (_pv(p_ref, v_ref, i) / l)
        o_ref[rs, :] = (o * _rms_scale(o, SUBLN_EPS) * gain).astype(o_ref.dtype)


def _attn_call(kernel, name, batch, seq, heads, ins, in_specs, extra_scratch=()):
    return pl.pallas_call(
        kernel,
        grid=(batch, heads),
        in_specs=in_specs,
        out_specs=pl.BlockSpec((seq, HEAD_DIM), lambda b, h: (b, h)),
        out_shape=jax.ShapeDtypeStruct((batch * seq, heads * HEAD_DIM), jnp.bfloat16),
        scratch_shapes=[pltpu.VMEM((seq, seq), jnp.float32), pltpu.VMEM((seq, seq), jnp.bfloat16),
                        *extra_scratch],
        compiler_params=_cparams(("parallel", "parallel")),
        name=name,
    )(*ins)


def _colblk(seq, width, col0):
    assert col0 % width == 0
    return pl.BlockSpec((seq, width), lambda b, h: (b, col0 // width + h))


def _diff_attn(l, z, lam, gain, lambda_init, batch, seq):
    kern = functools.partial(_diff_attn_kernel, lambda_init)
    specs = [_colblk(seq, HEAD_DIM, Z_AQ), _colblk(seq, HEAD_DIM, Z_AK), _colblk(seq, HEAD_DIM, Z_AV),
             _layer_spec(l, (4, DIFF_QK_DIM)), _layer_spec(l, (1, HEAD_DIM))]
    return _attn_call(kern, "diff_attn", batch, seq, DIFF_HEADS, (z, z, z, lam, gain), specs,
                      extra_scratch=(pltpu.VMEM((seq, HEAD_DIM), jnp.float32),))


def _mla_attn(q, k, v, batch, seq):
    specs = [_colblk(seq, MLA_QK_PAD, 0), _colblk(seq, MLA_QK_PAD, 0), _colblk(seq, HEAD_DIM, 0)]
    return _attn_call(_mla_attn_kernel, "mla_attn", batch, seq, MLA_HEADS, (q, k, v), specs)


def _fox_attn(z, ccol, crow, batch, seq):
    specs = []
    for col0 in (Z_FQ, Z_FK, Z_FV):
        specs += [_colblk(seq, LANES, col0 - FOX_SHIFT), _colblk(seq, LANES, col0 - FOX_SHIFT + LANES)]
    specs += [pl.BlockSpec((seq, LANES), lambda b, h: (b, 0)),
              pl.BlockSpec((1, SUBLANES, seq), lambda b, h: (b, 0, 0))]
    kv = pltpu.VMEM((seq, HEAD_DIM), jnp.bfloat16)
    return _attn_call(_fox_attn_kernel, "fox_attn", batch, seq, FOX_HEADS, (z,) * 6 + (ccol, crow), specs,
                      extra_scratch=(kv, kv))


def _out_proj_kernel(x_ref, a_ref, b_ref, c_ref, wa_ref, wb_ref, wc_ref, o_ref):
    acc = jnp.dot(a_ref[...], wa_ref[...], preferred_element_type=jnp.float32)
    acc = acc + jnp.dot(b_ref[...], wb_ref[...], preferred_element_type=jnp.float32)
    acc = acc + jnp.dot(c_ref[...], wc_ref[...], preferred_element_type=jnp.float32)
    o_ref[...] = x_ref[...] + acc


def _out_proj(l, x, oa, ob, oc, wa, wb, wc):
    m, d = x.shape
    lhs = lambda a: pl.BlockSpec((TM_PROJ, a.shape[1]), lambda i, j: (i, 0))
    rhs = lambda w: _layer_spec(l, (w.shape[1], TN_OUT), lambda i, j: j)
    xo = pl.BlockSpec((TM_PROJ, TN_OUT), lambda i, j: (i, j))
    return pl.pallas_call(
        _out_proj_kernel,
        grid=(m // TM_PROJ, d // TN_OUT),
        in_specs=[xo, lhs(oa), lhs(ob), lhs(oc), rhs(wa), rhs(wb), rhs(wc)],
        out_specs=xo,
        out_shape=jax.ShapeDtypeStruct((m, d), jnp.float32),
        compiler_params=_cparams(("parallel", "parallel")),
        name="out_proj",
    )(x, oa, ob, oc, wa, wb, wc)


FF_ROW_CHUNKS = 8


def _ffn_up_kernel(tiles_per_seq, x_ref, g_ref, wa_ref, wg_ref, cwa_ref, cwg_ref, cba_ref, cbg_ref,
                   o_ref, h_ref, ua_ref, ug_ref, carry_ref):
    i, j = pl.program_id(0), pl.program_id(1)
    tm = x_ref.shape[0]
    rc = tm // FF_ROW_CHUNKS

    @pl.when(j == 0)
    def _():
        x = x_ref[...]
        h_ref[...] = (x * _rms_scale(x, NORM_EPS) * g_ref[...]).astype(jnp.bfloat16)

    @pl.when(i % tiles_per_seq == 0)
    def _():
        carry_ref[j] = jnp.zeros(carry_ref.shape[1:], jnp.float32)

    ua_ref[0:SUBLANES, :] = carry_ref[j, 0]
    ug_ref[0:SUBLANES, :] = carry_ref[j, 1]

    def conv(u_ref, r0, rows, cw, cb):
        base = SUBLANES + r0
        return (u_ref[base:base + rows, :] * cw[2:3] + u_ref[base - 1:base - 1 + rows, :] * cw[1:2]
                + u_ref[base - 2:base - 2 + rows, :] * cw[0:1] + cb)

    ug = jnp.dot(h_ref[...], wg_ref[...], preferred_element_type=jnp.float32)
    ug_ref[SUBLANES:, :] = ug
    carry_ref[j, 1] = ug[tm - SUBLANES:, :]
    cwa, cwg, cba, cbg = cwa_ref[...], cwg_ref[...], cba_ref[...], cbg_ref[...]
    for r in range(FF_ROW_CHUNKS):
        r0 = r * rc
        ua = jnp.dot(h_ref[r0:r0 + rc, :], wa_ref[...], preferred_element_type=jnp.float32)
        ua_ref[SUBLANES + r0:SUBLANES + r0 + rc, :] = ua
        if r == FF_ROW_CHUNKS - 1:
            carry_ref[j, 0] = ua[rc - SUBLANES:, :]
        g = conv(ug_ref, r0, rc, cwg, cbg)
        a = conv(ua_ref, r0, rc, cwa, cba)
        o_ref[r0:r0 + rc, :] = (g / (1.0 + jnp.exp(-g)) * a).astype(o_ref.dtype)


def _ffn_up(l, x, gain, w_up, conv_w, conv_b, seq):
    m, d = x.shape
    nj = D_FF // TN_FF
    kern = functools.partial(_ffn_up_kernel, seq // TM_PROJ)
    col = lambda rows, off: _layer_spec(l, (rows, TN_FF), lambda i, j: j + off)
    return pl.pallas_call(
        kern,
        grid=(m // TM_PROJ, nj),
        in_specs=[pl.BlockSpec((TM_PROJ, d), lambda i, j: (i, 0)),
                  _layer_spec(l, (1, d)),
                  col(d, 0), col(d, nj), col(3, 0), col(3, nj), col(1, 0), col(1, nj)],
        out_specs=pl.BlockSpec((TM_PROJ, TN_FF), lambda i, j: (i, j)),
        out_shape=jax.ShapeDtypeStruct((m, D_FF), jnp.bfloat16),
        scratch_shapes=[pltpu.VMEM((TM_PROJ, d), jnp.bfloat16),
                        pltpu.VMEM((TM_PROJ + SUBLANES, TN_FF), jnp.float32),
                        pltpu.VMEM((TM_PROJ + SUBLANES, TN_FF), jnp.float32),
                        pltpu.VMEM((nj, 2, SUBLANES, TN_FF), jnp.float32)],
        compiler_params=_cparams(("arbitrary", "arbitrary")),
        name="ffn_up",
    )(x, gain, w_up, w_up, conv_w, conv_w, conv_b, conv_b)


def _ffn_down_kernel(x_ref, a_ref, w_ref, o_ref):
    o_ref[...] = x_ref[...] + jnp.dot(a_ref[...], w_ref[...], preferred_element_type=jnp.float32)


def _ffn_down(l, x, act, w):
    m, d = x.shape
    k = act.shape[1]
    xo = pl.BlockSpec((TM_PROJ, TN_DOWN), lambda i, j: (i, j))
    return pl.pallas_call(
        _ffn_down_kernel,
        grid=(m // TM_PROJ, d // TN_DOWN),
        in_specs=[xo, pl.BlockSpec((TM_PROJ, k), lambda i, j: (i, 0)),
                  _layer_spec(l, (k, TN_DOWN), lambda i, j: j)],
        out_specs=xo,
        out_shape=jax.ShapeDtypeStruct((m, d), jnp.float32),
        compiler_params=_cparams(("parallel", "parallel")),
        name="ffn_down",
    )(x, act, w)


def _final_norm_kernel(x_ref, g_ref, o_ref):
    x = x_ref[...]
    o_ref[...] = x * _rms_scale(x, NORM_EPS) * g_ref[...]


def _final_norm(x, gain):
    m, d = x.shape
    blk = pl.BlockSpec((TM_PREP, d), lambda i: (i, 0))
    return pl.pallas_call(
        _final_norm_kernel,
        grid=(m // TM_PREP,),
        in_specs=[blk, pl.BlockSpec((1, d), lambda i: (0, 0))],
        out_specs=blk,
        out_shape=jax.ShapeDtypeStruct((m, d), jnp.float32),
        compiler_params=_cparams(("parallel",)),
        name="final_norm",
    )(x, gain)


def _pad_w_uq(w):
    dp, r, _ = w.shape
    w = w.reshape(dp, r, MLA_HEADS, MLA_QK_NOPE + MLA_QK_ROPE)
    w = jnp.pad(w, ((0, 0), (0, 0), (0, 0), (0, MLA_QK_PAD - MLA_QK_NOPE - MLA_QK_ROPE)))
    return w.reshape(dp, r, MLA_HEADS * MLA_QK_PAD).astype(jnp.bfloat16)


def _split_w_ukv(w):
    dp, r, _ = w.shape
    w = w.reshape(dp, r, MLA_HEADS, MLA_QK_NOPE + HEAD_DIM)
    wk = w[..., :MLA_QK_NOPE].reshape(dp, r, MLA_HEADS * MLA_QK_NOPE)
    wv = w[..., MLA_QK_NOPE:].reshape(dp, r, MLA_HEADS * HEAD_DIM)
    return wk.astype(jnp.bfloat16), wv.astype(jnp.bfloat16)


def kernel(x, positions, attn_norm, w_in, diff_lambda, diff_out_norm, mla_q_norm, mla_kv_norm, mla_w_uq,
           mla_w_ukv, fox_forget_bias, w_o, ffn_norm, ffn_w_up, ffn_conv_w, ffn_conv_b, ffn_w_down,
           final_norm):
    batch, seq, d = x.shape
    depth = w_in.shape[0]
    m = batch * seq
    assert seq % TQ == 0 and seq % TM_PROJ == 0 and m % TM_PROJ == 0
    assert w_in.shape[2] == IN_WIDTH

    pat_a, pat_b = _rope_lane_patterns()
    pos_col = positions.reshape(m, 1)
    tabs_a = _rope_tables(pos_col, *pat_a)
    tabs_b = _rope_tables(pos_col, *pat_b)

    bf = lambda a: a.astype(jnp.bfloat16)
    rows = lambda a: a.reshape(depth, 1, -1)
    n_a, n_b = DIFF_HEADS * HEAD_DIM, MLA_HEADS * HEAD_DIM
    w_in_b = jnp.pad(bf(jnp.transpose(w_in, (0, 2, 1))), ((0, 0), (0, Z_WIDTH - IN_WIDTH), (0, 0)))
    w_uq_b = _pad_w_uq(mla_w_uq)
    w_uk_b, w_uv_b = _split_w_ukv(mla_w_ukv)
    wo_a, wo_b, wo_c = bf(w_o[:, :n_a]), bf(w_o[:, n_a:n_a + n_b]), bf(w_o[:, n_a + n_b:])
    w_up_b, w_down_b = bf(ffn_w_up), bf(ffn_w_down)
    fb = jnp.pad(fox_forget_bias, ((0, 0), (FOX_SHIFT, LANES - FOX_SHIFT - FOX_HEADS))).reshape(depth, 1, LANES)
    g_attn, g_ffn, g_q, g_kv, g_diff = (rows(attn_norm), rows(ffn_norm), rows(mla_q_norm),
                                        rows(mla_kv_norm), rows(diff_out_norm))
    conv_b = rows(ffn_conv_b)

    xf = x.reshape(m, d)
    for l in range(depth):
        lambda_init = 0.8 - 0.6 * math.exp(-0.3 * l)
        z, zg = _in_proj(l, xf, g_attn, w_in_b, tabs_a)
        q_b, k_b, v_b = _mla_up(l, z, g_q, g_kv, w_uq_b, w_uk_b, w_uv_b, tabs_b)
        ccol, crow = _fox_gate(l, zg, fb, batch, seq)
        o_a = _diff_attn(l, z, diff_lambda, g_diff, lambda_init, batch, seq)
        o_b = _mla_attn(q_b, k_b, v_b, batch, seq)
        o_c = _fox_attn(z, ccol, crow, batch, seq)
        xf = _out_proj(l, xf, o_a, o_b, o_c, wo_a, wo_b, wo_c)
        act = _ffn_up(l, xf, g_ffn, w_up_b, ffn_conv_w, conv_b, seq)
        xf = _ffn_down(l, xf, act, w_down_b)
    return _final_norm(xf, final_norm.reshape(1, d)).reshape(batch, seq, d)
```

```python
import functools
import math

import numpy as np
import jax
import jax.numpy as jnp
from jax import lax
from jax.experimental import pallas as pl
from jax.experimental.pallas import tpu as pltpu

HEAD_DIM = 128
DIFF_HEADS = 4
DIFF_QK_DIM = 64
MLA_HEADS = 6
MLA_LORA = 512
MLA_QK_NOPE = 128
MLA_QK_ROPE = 64
FOX_HEADS = 6
D_FF = 5632
ROPE_THETA = 500000.0
PARTIAL_ROT_DIM = DIFF_QK_DIM // 4
NORM_EPS = 1e-6
SUBLN_EPS = 1e-5
LOG2E = math.log2(math.e)

LANES = 128
SUBLANES = 8
VMEM_LIMIT_BYTES = 56 * 1024 * 1024

Z_AQ, Z_AK, Z_AV = 0, 512, 1024
Z_CQ, Z_CKV, Z_KR = 1536, 2048, 2560
Z_FQ, Z_FK, Z_FV = 2624, 3392, 4160
Z_FG = 4928
IN_WIDTH = Z_FG + FOX_HEADS
FOX_SHIFT = Z_FQ % LANES
assert Z_FK % LANES == FOX_SHIFT and Z_FV % LANES == FOX_SHIFT and Z_FG % LANES == FOX_SHIFT
MLA_QK_PAD = 256

TM_PROJ = 1024
TN_IN = 1024
TM_OUT = 512
TN_OUT = 2048
TN_FF = 512
TN_DOWN = 512
TM_PREP = 512
TQ = 256
Z_WIDTH = -(-IN_WIDTH // TN_IN) * TN_IN


def _cparams(sem):
    return pltpu.CompilerParams(dimension_semantics=sem, vmem_limit_bytes=VMEM_LIMIT_BYTES)


def _rms_scale(x32, eps):
    return lax.rsqrt(jnp.mean(x32 * x32, axis=-1, keepdims=True) + eps)


def _layer_spec(l, shape, col=None):
    if col is None:
        return pl.BlockSpec((None,) + shape, lambda *g: (l, 0, 0))
    return pl.BlockSpec((None,) + shape, lambda *g: (l, 0, col(*g)))


def _rope_table_kernel(pos_ref, invf_ref, m1_ref, m2_ref, c_ref, s1_ref, s2_ref):
    ang = pos_ref[...].astype(jnp.float32) * invf_ref[...]
    sn = jnp.sin(ang)
    c_ref[...] = jnp.cos(ang)
    s1_ref[...] = sn * m1_ref[...]
    s2_ref[...] = sn * m2_ref[...]


def _rope_tables(pos_col, invf, m1, m2):
    m = pos_col.shape[0]
    row = pl.BlockSpec((1, LANES), lambda i: (0, 0))
    tab = pl.BlockSpec((TM_PREP, LANES), lambda i: (i, 0))
    return pl.pallas_call(
        _rope_table_kernel,
        grid=(m // TM_PREP,),
        in_specs=[pl.BlockSpec((TM_PREP, 1), lambda i: (i, 0)), row, row, row],
        out_specs=[tab, tab, tab],
        out_shape=[jax.ShapeDtypeStruct((m, LANES), jnp.float32)] * 3,
        compiler_params=_cparams(("parallel",)),
        name="rope_tables",
    )(pos_col, invf, m1, m2)


def _rope_lane_patterns():
    lane = np.arange(LANES)
    half_a = PARTIAL_ROT_DIM // 2
    r = lane % DIFF_QK_DIM
    inv_a = ROPE_THETA ** (-jnp.arange(0, PARTIAL_ROT_DIM, 2, dtype=jnp.float32) / PARTIAL_ROT_DIM)
    invf_a = jnp.where(jnp.asarray(r < PARTIAL_ROT_DIM), inv_a[jnp.asarray(r % half_a)], 0.0)
    m1_a = np.where(r < half_a, -1.0, 0.0)
    m2_a = np.where((r >= half_a) & (r < PARTIAL_ROT_DIM), 1.0, 0.0)
    half_b = MLA_QK_ROPE // 2
    inv_b = ROPE_THETA ** (-jnp.arange(0, MLA_QK_ROPE, 2, dtype=jnp.float32) / MLA_QK_ROPE)
    invf_b = jnp.where(jnp.asarray(lane < MLA_QK_ROPE), inv_b[jnp.asarray(lane % half_b)], 0.0)
    m1_b = np.where(lane < half_b, -1.0, 0.0)
    m2_b = np.where((lane >= half_b) & (lane < MLA_QK_ROPE), 1.0, 0.0)
    f = lambda a: jnp.asarray(a, jnp.float32).reshape(1, LANES)
    return (f(invf_a), f(m1_a), f(m2_a)), (f(invf_b), f(m1_b), f(m2_b))


def _rope_block(x, c, s1, s2, d):
    return x * c + pltpu.roll(x, LANES - d, 1) * s1 + pltpu.roll(x, d, 1) * s2


def _in_proj_kernel(x_ref, g_ref, w_ref, c_ref, s1_ref, s2_ref, z_ref, zg_ref, h_ref):
    j = pl.program_id(1)

    @pl.when(j == 0)
    def _():
        x = x_ref[...]
        h_ref[...] = (x * _rms_scale(x, NORM_EPS) * g_ref[...]).astype(jnp.bfloat16)

    @pl.when(j == 0)
    def _():
        c, s1, s2 = c_ref[...], s1_ref[...], s2_ref[...]
        scale = DIFF_QK_DIM ** -0.5 * LOG2E
        acc = jnp.dot(h_ref[...], w_ref[...], preferred_element_type=jnp.float32)
        for b in range(TN_IN // LANES):
            sl = slice(b * LANES, (b + 1) * LANES)
            r = _rope_block(acc[:, sl], c, s1, s2, PARTIAL_ROT_DIM // 2)
            z_ref[:, sl] = ((r * scale) if b * LANES < Z_AK else r).astype(jnp.bfloat16)

    @pl.when(j != 0)
    def _():
        acc = jnp.dot(h_ref[...], w_ref[...], preferred_element_type=jnp.float32)
        z_ref[...] = acc.astype(jnp.bfloat16)

        @pl.when(j == Z_FG // TN_IN)
        def _():
            off = Z_FG % TN_IN // LANES * LANES
            zg_ref[...] = acc[:, off:off + LANES]


def _in_proj(l, x, gain, w, tabs):
    m, d = x.shape
    assert Z_AQ == 0 and TN_IN == Z_AV
    tab = pl.BlockSpec((TM_PROJ, LANES), lambda i, j: (i, 0))
    return pl.pallas_call(
        _in_proj_kernel,
        grid=(m // TM_PROJ, Z_WIDTH // TN_IN),
        in_specs=[pl.BlockSpec((TM_PROJ, d), lambda i, j: (i, 0)),
                  _layer_spec(l, (1, d)),
                  _layer_spec(l, (d, TN_IN), lambda i, j: j),
                  tab, tab, tab],
        out_specs=[pl.BlockSpec((TM_PROJ, TN_IN), lambda i, j: (i, j)),
                   pl.BlockSpec((TM_PROJ, LANES), lambda i, j: (i, 0))],
        out_shape=[jax.ShapeDtypeStruct((m, Z_WIDTH), jnp.bfloat16),
                   jax.ShapeDtypeStruct((m, LANES), jnp.float32)],
        scratch_shapes=[pltpu.VMEM((TM_PROJ, d), jnp.bfloat16)],
        compiler_params=_cparams(("parallel", "arbitrary")),
        name="in_proj",
    )(x, gain, w, *tabs)


def _mla_up_kernel(cq_ref, ckv_ref, kr_ref, gq_ref, gkv_ref, wq_ref, wk_ref, wv_ref,
                   c_ref, s1_ref, s2_ref, q_ref, k_ref, v_ref):
    c, s1, s2 = c_ref[...], s1_ref[...], s2_ref[...]
    half = MLA_QK_ROPE // 2
    scale = (MLA_QK_NOPE + MLA_QK_ROPE) ** -0.5 * LOG2E

    cq = cq_ref[...].astype(jnp.float32)
    cqn = (cq * _rms_scale(cq, NORM_EPS) * gq_ref[...]).astype(jnp.bfloat16)
    q = jnp.dot(cqn, wq_ref[...], preferred_element_type=jnp.float32)
    ckv = ckv_ref[...].astype(jnp.float32)
    ckvn = (ckv * _rms_scale(ckv, NORM_EPS) * gkv_ref[...]).astype(jnp.bfloat16)
    kn = jnp.dot(ckvn, wk_ref[...], preferred_element_type=jnp.float32)
    v_ref[...] = jnp.dot(ckvn, wv_ref[...], preferred_element_type=jnp.float32).astype(jnp.bfloat16)
    kr = _rope_block(kr_ref[...].astype(jnp.float32), c, s1, s2, half)
    lane = lax.broadcasted_iota(jnp.int32, kr.shape, 1)
    kr = jnp.where(lane < MLA_QK_ROPE, kr, 0.0).astype(jnp.bfloat16)

    for h in range(MLA_HEADS):
        base = h * MLA_QK_PAD
        q_ref[:, base:base + LANES] = (q[:, base:base + LANES] * scale).astype(jnp.bfloat16)
        qr = _rope_block(q[:, base + LANES:base + 2 * LANES], c, s1, s2, half)
        q_ref[:, base + LANES:base + 2 * LANES] = (qr * scale).astype(jnp.bfloat16)
        k_ref[:, base:base + LANES] = kn[:, h * LANES:(h + 1) * LANES].astype(jnp.bfloat16)
        k_ref[:, base + LANES:base + 2 * LANES] = kr


def _mla_up(l, z, gq, gkv, wq, wk, wv, tabs):
    m = z.shape[0]
    hq = MLA_HEADS * MLA_QK_PAD
    hv = MLA_HEADS * HEAD_DIM
    tab = pl.BlockSpec((TM_PREP, LANES), lambda i: (i, 0))
    return pl.pallas_call(
        _mla_up_kernel,
        grid=(m // TM_PREP,),
        in_specs=[pl.BlockSpec((TM_PREP, MLA_LORA), lambda i: (i, Z_CQ // MLA_LORA)),
                  pl.BlockSpec((TM_PREP, MLA_LORA), lambda i: (i, Z_CKV // MLA_LORA)),
                  pl.BlockSpec((TM_PREP, LANES), lambda i: (i, Z_KR // LANES)),
                  _layer_spec(l, (1, MLA_LORA)), _layer_spec(l, (1, MLA_LORA)),
                  _layer_spec(l, (MLA_LORA, hq)), _layer_spec(l, (MLA_LORA, hv)),
                  _layer_spec(l, (MLA_LORA, hv)),
                  tab, tab, tab],
        out_specs=[pl.BlockSpec((TM_PREP, hq), lambda i: (i, 0)),
                   pl.BlockSpec((TM_PREP, hq), lambda i: (i, 0)),
                   pl.BlockSpec((TM_PREP, hv), lambda i: (i, 0))],
        out_shape=[jax.ShapeDtypeStruct((m, hq), jnp.bfloat16),
                   jax.ShapeDtypeStruct((m, hq), jnp.bfloat16),
                   jax.ShapeDtypeStruct((m, hv), jnp.bfloat16)],
        compiler_params=_cparams(("parallel",)),
        name="mla_up",
    )(z, z, z, gq, gkv, wq, wk, wv, *tabs)


def _fox_gate_kernel(zg_ref, fb_ref, col_ref, row_ref):
    x = zg_ref[...] + fb_ref[...]
    logf = jnp.minimum(x, 0.0) - jnp.log(1.0 + jnp.exp(-jnp.abs(x)))
    s = logf.shape[0]
    row = lax.broadcasted_iota(jnp.int32, logf.shape, 0)
    d = 1
    while d < s:
        logf = logf + jnp.where(row >= d, pltpu.roll(logf, d, 0), 0.0)
        d *= 2
    logf = logf * LOG2E
    col_ref[...] = logf
    row_ref[0] = logf.T[FOX_SHIFT:FOX_SHIFT + SUBLANES, :]


def _fox_gate(l, zg, fb, batch, seq):
    return pl.pallas_call(
        _fox_gate_kernel,
        grid=(batch,),
        in_specs=[pl.BlockSpec((seq, LANES), lambda b: (b, 0)),
                  _layer_spec(l, (1, LANES))],
        out_specs=[pl.BlockSpec((seq, LANES), lambda b: (b, 0)),
                   pl.BlockSpec((1, SUBLANES, seq), lambda b: (b, 0, 0))],
        out_shape=[jax.ShapeDtypeStruct((batch * seq, LANES), jnp.float32),
                   jax.ShapeDtypeStruct((batch, SUBLANES, seq), jnp.float32)],
        compiler_params=_cparams(("parallel",)),
        name="fox_gate",
    )(zg, fb)


_NT = (((1,), (1,)), ((), ()))


def _scores(q, k_ref, s_ref, bias_fn=None):
    n = s_ref.shape[0] // TQ
    for c in range(n):
        cs = slice(c * TQ, (c + 1) * TQ)
        s = lax.dot_general(q[c * TQ:, :], k_ref[cs, :], _NT, preferred_element_type=jnp.float32)
        if bias_fn is not None:
            s = s + bias_fn(c)
        s_ref[c * TQ:, cs] = s


def _softmax_tile(s_ref, p_ref, i):
    rs = slice(i * TQ, (i + 1) * TQ)

    def chunk(c):
        s = s_ref[rs, c * TQ:(c + 1) * TQ]
        if c == i:
            rows = lax.broadcasted_iota(jnp.int32, s.shape, 0)
            cols = lax.broadcasted_iota(jnp.int32, s.shape, 1)
            s = jnp.where(cols <= rows, s, -jnp.inf)
        return s

    mpart = None
    for c in range(i + 1):
        s = chunk(c)
        for b in range(TQ // LANES):
            blk = s[:, b * LANES:(b + 1) * LANES]
            mpart = blk if mpart is None else jnp.maximum(mpart, blk)
    m = jnp.max(mpart, axis=1, keepdims=True)
    for c in range(i + 1):
        p_ref[rs, c * TQ:(c + 1) * TQ] = jnp.exp2(chunk(c) - m).astype(jnp.bfloat16)


def _fill_values(va_ref, v):
    va_ref[:, :HEAD_DIM] = v
    va_ref[:, HEAD_DIM:] = jnp.ones((va_ref.shape[0], va_ref.shape[1] - HEAD_DIM), va_ref.dtype)


def _attend(p_ref, va_ref, i):
    n = (i + 1) * TQ
    o = jnp.dot(p_ref[i * TQ:(i + 1) * TQ, :n], va_ref[:n, :], preferred_element_type=jnp.float32)
    return o[:, :HEAD_DIM] / o[:, HEAD_DIM:]


def _mla_attn_kernel(q_ref, k_ref, v_ref, o_ref, s_ref, p_ref, va_ref):
    _fill_values(va_ref, v_ref[...])
    _scores(q_ref[...], k_ref, s_ref)
    for i in range(q_ref.shape[0] // TQ):
        _softmax_tile(s_ref, p_ref, i)
        o_ref[i * TQ:(i + 1) * TQ, :] = _attend(p_ref, va_ref, i).astype(o_ref.dtype)


def _unshift(lo, hi):
    return jnp.concatenate([lo[:, FOX_SHIFT:], hi[:, :FOX_SHIFT]], axis=1)


def _fox_attn_kernel(qlo_ref, qhi_ref, klo_ref, khi_ref, vlo_ref, vhi_ref, ccol_ref, crow_ref,
                     o_ref, s_ref, p_ref, va_ref, k_ref):
    h = pl.program_id(1)
    scale = HEAD_DIM ** -0.5 * LOG2E
    k_ref[...] = _unshift(klo_ref[...], khi_ref[...])
    _fill_values(va_ref, _unshift(vlo_ref[...], vhi_ref[...]))
    q = _unshift(qlo_ref[...], qhi_ref[...])
    q = (q.astype(jnp.float32) * scale).astype(jnp.bfloat16)
    crow = crow_ref[0, pl.ds(h, 1), :]
    cc = ccol_ref[...]
    lane = lax.broadcasted_iota(jnp.int32, cc.shape, 1)
    ccq = jnp.sum(jnp.where(lane == h + FOX_SHIFT, cc, 0.0), axis=1, keepdims=True)
    _scores(q, k_ref, s_ref, bias_fn=lambda c: ccq[c * TQ:, :] - crow[:, c * TQ:(c + 1) * TQ])
    for i in range(o_ref.shape[0] // TQ):
        _softmax_tile(s_ref, p_ref, i)
        o_ref[i * TQ:(i + 1) * TQ, :] = _attend(p_ref, va_ref, i).astype(o_ref.dtype)


def _diff_attn_kernel(lambda_init, q_ref, k_ref, v_ref, lam_ref, g_ref, o_ref, s_ref, p_ref, va_ref, o1_ref):
    lv = lam_ref[...]
    lam = (jnp.exp(jnp.sum(lv[0:1] * lv[1:2], axis=1, keepdims=True))
           - jnp.exp(jnp.sum(lv[2:3] * lv[3:4], axis=1, keepdims=True)) + lambda_init)
    gain = g_ref[...] * (1.0 - lambda_init)
    q = q_ref[...]
    lane = lax.broadcasted_iota(jnp.int32, q.shape, 1)
    zero = jnp.zeros_like(q)
    nq = q_ref.shape[0] // TQ
    _fill_values(va_ref, v_ref[...])
    _scores(jnp.where(lane < DIFF_QK_DIM, q, zero), k_ref, s_ref)
    for i in range(nq):
        _softmax_tile(s_ref, p_ref, i)
        o1_ref[i * TQ:(i + 1) * TQ, :] = _attend(p_ref, va_ref, i)
    _scores(jnp.where(lane >= DIFF_QK_DIM, q, zero), k_ref, s_ref)
    for i in range(nq):
        rs = slice(i * TQ, (i + 1) * TQ)
        _softmax_tile(s_ref, p_ref, i)
        o = o1_ref[rs, :] - lam * _attend(p_ref, va_ref, i)
        o_ref[rs, :] = (o * _rms_scale(o, SUBLN_EPS) * gain).astype(o_ref.dtype)


def _attn_call(kernel, name, batch, seq, heads, ins, in_specs, extra_scratch=()):
    return pl.pallas_call(
        kernel,
        grid=(batch, heads),
        in_specs=in_specs,
        out_specs=pl.BlockSpec((seq, HEAD_DIM), lambda b, h: (b, h)),
        out_shape=jax.ShapeDtypeStruct((batch * seq, heads * HEAD_DIM), jnp.bfloat16),
        scratch_shapes=[pltpu.VMEM((seq, seq), jnp.float32), pltpu.VMEM((seq, seq), jnp.bfloat16),
                        pltpu.VMEM((seq, 2 * HEAD_DIM), jnp.bfloat16), *extra_scratch],
        compiler_params=_cparams(("parallel", "parallel")),
        name=name,
    )(*ins)


def _colblk(seq, width, col0):
    assert col0 % width == 0
    return pl.BlockSpec((seq, width), lambda b, h: (b, col0 // width + h))


def _diff_attn(l, z, lam, gain, lambda_init, batch, seq):
    kern = functools.partial(_diff_attn_kernel, lambda_init)
    specs = [_colblk(seq, HEAD_DIM, Z_AQ), _colblk(seq, HEAD_DIM, Z_AK), _colblk(seq, HEAD_DIM, Z_AV),
             _layer_spec(l, (4, DIFF_QK_DIM)), _layer_spec(l, (1, HEAD_DIM))]
    return _attn_call(kern, "diff_attn", batch, seq, DIFF_HEADS, (z, z, z, lam, gain), specs,
                      extra_scratch=(pltpu.VMEM((seq, HEAD_DIM), jnp.float32),))


def _mla_attn(q, k, v, batch, seq):
    specs = [_colblk(seq, MLA_QK_PAD, 0), _colblk(seq, MLA_QK_PAD, 0), _colblk(seq, HEAD_DIM, 0)]
    return _attn_call(_mla_attn_kernel, "mla_attn", batch, seq, MLA_HEADS, (q, k, v), specs)


def _fox_attn(z, ccol, crow, batch, seq):
    specs = []
    for col0 in (Z_FQ, Z_FK, Z_FV):
        specs += [_colblk(seq, LANES, col0 - FOX_SHIFT), _colblk(seq, LANES, col0 - FOX_SHIFT + LANES)]
    specs += [pl.BlockSpec((seq, LANES), lambda b, h: (b, 0)),
              pl.BlockSpec((1, SUBLANES, seq), lambda b, h: (b, 0, 0))]
    kv = pltpu.VMEM((seq, HEAD_DIM), jnp.bfloat16)
    return _attn_call(_fox_attn_kernel, "fox_attn", batch, seq, FOX_HEADS, (z,) * 6 + (ccol, crow), specs,
                      extra_scratch=(kv,))


def _out_proj_kernel(x_ref, a_ref, b_ref, c_ref, wa_ref, wb_ref, wc_ref, o_ref):
    acc = jnp.dot(a_ref[...], wa_ref[...], preferred_element_type=jnp.float32)
    acc = acc + jnp.dot(b_ref[...], wb_ref[...], preferred_element_type=jnp.float32)
    acc = acc + jnp.dot(c_ref[...], wc_ref[...], preferred_element_type=jnp.float32)
    o_ref[...] = x_ref[...] + acc


def _out_proj(l, x, oa, ob, oc, wa, wb, wc):
    m, d = x.shape
    lhs = lambda a: pl.BlockSpec((TM_OUT, a.shape[1]), lambda i, j: (i, 0))
    rhs = lambda w: _layer_spec(l, (w.shape[1], TN_OUT), lambda i, j: j)
    xo = pl.BlockSpec((TM_OUT, TN_OUT), lambda i, j: (i, j))
    return pl.pallas_call(
        _out_proj_kernel,
        grid=(m // TM_OUT, d // TN_OUT),
        in_specs=[xo, lhs(oa), lhs(ob), lhs(oc), rhs(wa), rhs(wb), rhs(wc)],
        out_specs=xo,
        out_shape=jax.ShapeDtypeStruct((m, d), jnp.float32),
        compiler_params=_cparams(("parallel", "parallel")),
        name="out_proj",
    )(x, oa, ob, oc, wa, wb, wc)


def _ffn_up_kernel(tiles_per_seq, x_ref, g_ref, wa_ref, wg_ref, cwa_ref, cwg_ref, cba_ref, cbg_ref,
                   o_ref, h_ref, ua_ref, ug_ref, carry_ref):
    i, j = pl.program_id(0), pl.program_id(1)
    tm = x_ref.shape[0]

    @pl.when(j == 0)
    def _():
        x = x_ref[...]
        h_ref[...] = (x * _rms_scale(x, NORM_EPS) * g_ref[...]).astype(jnp.bfloat16)

    @pl.when(i % tiles_per_seq == 0)
    def _():
        carry_ref[j] = jnp.zeros(carry_ref.shape[1:], jnp.float32)

    ua_ref[0:SUBLANES, :] = carry_ref[j, 0]
    ug_ref[0:SUBLANES, :] = carry_ref[j, 1]

    def conv(u_ref, w_ref, half, cw_ref, cb_ref):
        u = jnp.dot(h_ref[...], w_ref[...], preferred_element_type=jnp.float32)
        u_ref[SUBLANES:, :] = u
        carry_ref[j, half] = u[tm - SUBLANES:, :]
        cw = cw_ref[...]
        return (u * cw[2:3] + u_ref[SUBLANES - 1:SUBLANES - 1 + tm, :] * cw[1:2]
                + u_ref[SUBLANES - 2:SUBLANES - 2 + tm, :] * cw[0:1] + cb_ref[...])

    g = conv(ug_ref, wg_ref, 1, cwg_ref, cbg_ref)
    gate = g / (1.0 + jnp.exp(-g))
    a = conv(ua_ref, wa_ref, 0, cwa_ref, cba_ref)
    o_ref[...] = (gate * a).astype(o_ref.dtype)


def _ffn_up(l, x, gain, w_up, conv_w, conv_b, seq):
    m, d = x.shape
    nj = D_FF // TN_FF
    kern = functools.partial(_ffn_up_kernel, seq // TM_PROJ)
    col = lambda rows, off: _layer_spec(l, (rows, TN_FF), lambda i, j: j + off)
    return pl.pallas_call(
        kern,
        grid=(m // TM_PROJ, nj),
        in_specs=[pl.BlockSpec((TM_PROJ, d), lambda i, j: (i, 0)),
                  _layer_spec(l, (1, d)),
                  col(d, 0), col(d, nj), col(3, 0), col(3, nj), col(1, 0), col(1, nj)],
        out_specs=pl.BlockSpec((TM_PROJ, TN_FF), lambda i, j: (i, j)),
        out_shape=jax.ShapeDtypeStruct((m, D_FF), jnp.bfloat16),
        scratch_shapes=[pltpu.VMEM((TM_PROJ, d), jnp.bfloat16),
                        pltpu.VMEM((TM_PROJ + SUBLANES, TN_FF), jnp.float32),
                        pltpu.VMEM((TM_PROJ + SUBLANES, TN_FF), jnp.float32),
                        pltpu.VMEM((nj, 2, SUBLANES, TN_FF), jnp.float32)],
        compiler_params=_cparams(("arbitrary", "arbitrary")),
        name="ffn_up",
    )(x, gain, w_up, w_up, conv_w, conv_w, conv_b, conv_b)


def _ffn_down_kernel(x_ref, a_ref, w_ref, o_ref):
    o_ref[...] = x_ref[...] + jnp.dot(a_ref[...], w_ref[...], preferred_element_type=jnp.float32)


def _ffn_down(l, x, act, w):
    m, d = x.shape
    k = act.shape[1]
    xo = pl.BlockSpec((TM_PROJ, TN_DOWN), lambda i, j: (i, j))
    return pl.pallas_call(
        _ffn_down_kernel,
        grid=(m // TM_PROJ, d // TN_DOWN),
        in_specs=[xo, pl.BlockSpec((TM_PROJ, k), lambda i, j: (i, 0)),
                  _layer_spec(l, (k, TN_DOWN), lambda i, j: j)],
        out_specs=xo,
        out_shape=jax.ShapeDtypeStruct((m, d), jnp.float32),
        compiler_params=_cparams(("parallel", "parallel")),
        name="ffn_down",
    )(x, act, w)


def _final_norm_kernel(x_ref, g_ref, o_ref):
    x = x_ref[...]
    o_ref[...] = x * _rms_scale(x, NORM_EPS) * g_ref[...]


def _final_norm(x, gain):
    m, d = x.shape
    blk = pl.BlockSpec((TM_PREP, d), lambda i: (i, 0))
    return pl.pallas_call(
        _final_norm_kernel,
        grid=(m // TM_PREP,),
        in_specs=[blk, pl.BlockSpec((1, d), lambda i: (0, 0))],
        out_specs=blk,
        out_shape=jax.ShapeDtypeStruct((m, d), jnp.float32),
        compiler_params=_cparams(("parallel",)),
        name="final_norm",
    )(x, gain)


def _cast_pad_kernel(w_ref, o_ref):
    n = w_ref.shape[1]
    full = n // LANES * LANES
    o_ref[:, :full] = w_ref[:, :full].astype(jnp.bfloat16)
    o_ref[:, full:] = jnp.zeros((o_ref.shape[0], o_ref.shape[1] - full), jnp.bfloat16)
    o_ref[:, full:n] = w_ref[:, full:].astype(jnp.bfloat16)


def _cast_pad(w, width, rows_per_step=256):
    dp, k, n = w.shape
    return pl.pallas_call(
        _cast_pad_kernel,
        grid=(dp, k // rows_per_step),
        in_specs=[pl.BlockSpec((None, rows_per_step, n), lambda l, r: (l, r, 0))],
        out_specs=pl.BlockSpec((None, rows_per_step, width), lambda l, r: (l, r, 0)),
        out_shape=jax.ShapeDtypeStruct((dp, k, width), jnp.bfloat16),
        compiler_params=_cparams(("parallel", "parallel")),
        name="cast_pad",
    )(w)


def _pad_w_uq(w):
    dp, r, _ = w.shape
    w = w.reshape(dp, r, MLA_HEADS, MLA_QK_NOPE + MLA_QK_ROPE)
    w = jnp.pad(w, ((0, 0), (0, 0), (0, 0), (0, MLA_QK_PAD - MLA_QK_NOPE - MLA_QK_ROPE)))
    return w.reshape(dp, r, MLA_HEADS * MLA_QK_PAD).astype(jnp.bfloat16)


def _split_w_ukv(w):
    dp, r, _ = w.shape
    w = w.reshape(dp, r, MLA_HEADS, MLA_QK_NOPE + HEAD_DIM)
    wk = w[..., :MLA_QK_NOPE].reshape(dp, r, MLA_HEADS * MLA_QK_NOPE)
    wv = w[..., MLA_QK_NOPE:].reshape(dp, r, MLA_HEADS * HEAD_DIM)
    return wk.astype(jnp.bfloat16), wv.astype(jnp.bfloat16)


def kernel(x, positions, attn_norm, w_in, diff_lambda, diff_out_norm, mla_q_norm, mla_kv_norm, mla_w_uq,
           mla_w_ukv, fox_forget_bias, w_o, ffn_norm, ffn_w_up, ffn_conv_w, ffn_conv_b, ffn_w_down,
           final_norm):
    batch, seq, d = x.shape
    depth = w_in.shape[0]
    m = batch * seq
    assert seq % TQ == 0 and seq % TM_PROJ == 0 and m % TM_PROJ == 0
    assert w_in.shape[2] == IN_WIDTH

    pat_a, pat_b = _rope_lane_patterns()
    pos_col = positions.reshape(m, 1)
    tabs_a = _rope_tables(pos_col, *pat_a)
    tabs_b = _rope_tables(pos_col, *pat_b)

    bf = lambda a: a.astype(jnp.bfloat16)
    rows = lambda a: a.reshape(depth, 1, -1)
    n_a, n_b = DIFF_HEADS * HEAD_DIM, MLA_HEADS * HEAD_DIM
    w_in_b = _cast_pad(w_in, Z_WIDTH)
    w_uq_b = _pad_w_uq(mla_w_uq)
    w_uk_b, w_uv_b = _split_w_ukv(mla_w_ukv)
    wo_a, wo_b, wo_c = bf(w_o[:, :n_a]), bf(w_o[:, n_a:n_a + n_b]), bf(w_o[:, n_a + n_b:])
    w_up_b, w_down_b = bf(ffn_w_up), bf(ffn_w_down)
    fb = jnp.pad(fox_forget_bias, ((0, 0), (FOX_SHIFT, LANES - FOX_SHIFT - FOX_HEADS))).reshape(depth, 1, LANES)
    g_attn, g_ffn, g_q, g_kv, g_diff = (rows(attn_norm), rows(ffn_norm), rows(mla_q_norm),
                                        rows(mla_kv_norm), rows(diff_out_norm))
    conv_b = rows(ffn_conv_b)

    xf = x.reshape(m, d)
    for l in range(depth):
        lambda_init = 0.8 - 0.6 * math.exp(-0.3 * l)
        z, zg = _in_proj(l, xf, g_attn, w_in_b, tabs_a)
        q_b, k_b, v_b = _mla_up(l, z, g_q, g_kv, w_uq_b, w_uk_b, w_uv_b, tabs_b)
        ccol, crow = _fox_gate(l, zg, fb, batch, seq)
        o_a = _diff_attn(l, z, diff_lambda, g_diff, lambda_init, batch, seq)
        o_b = _mla_attn(q_b, k_b, v_b, batch, seq)
        o_c = _fox_attn(z, ccol, crow, batch, seq)
        xf = _out_proj(l, xf, o_a, o_b, o_c, wo_a, wo_b, wo_c)
        act = _ffn_up(l, xf, g_ffn, w_up_b, ffn_conv_w, conv_b, seq)
        xf = _ffn_down(l, xf, act, w_down_b)
    return _final_norm(xf, final_norm.reshape(1, d)).reshape(batch, seq, d)
```

```python
import functools
import math

import numpy as np
import jax
import jax.numpy as jnp
from jax import lax
from jax.experimental import pallas as pl
from jax.experimental.pallas import tpu as pltpu

HEAD_DIM = 128
DIFF_HEADS = 4
DIFF_QK_DIM = 64
MLA_HEADS = 6
MLA_LORA = 512
MLA_QK_NOPE = 128
MLA_QK_ROPE = 64
FOX_HEADS = 6
D_FF = 5632
ROPE_THETA = 500000.0
PARTIAL_ROT_DIM = DIFF_QK_DIM // 4
NORM_EPS = 1e-6
SUBLN_EPS = 1e-5
LOG2E = math.log2(math.e)

LANES = 128
SUBLANES = 8
VMEM_LIMIT_BYTES = 56 * 1024 * 1024

Z_AQ, Z_AK, Z_AV = 0, 512, 1024
Z_CQ, Z_CKV, Z_KR = 1536, 2048, 2560
Z_FQ, Z_FK, Z_FV = 2624, 3392, 4160
Z_FG = 4928
IN_WIDTH = Z_FG + FOX_HEADS
FOX_SHIFT = Z_FQ % LANES
assert Z_FK % LANES == FOX_SHIFT and Z_FV % LANES == FOX_SHIFT and Z_FG % LANES == FOX_SHIFT
MLA_QK_PAD = 256

TM_PROJ = 1024
TN_IN = 1024
TM_OUT = 512
TN_OUT = 2048
TN_FF = 512
TN_DOWN = 512
TM_PREP = 512
TQ = 256
Z_WIDTH = -(-IN_WIDTH // TN_IN) * TN_IN


def _cparams(sem):
    return pltpu.CompilerParams(dimension_semantics=sem, vmem_limit_bytes=VMEM_LIMIT_BYTES)


def _rms_scale(x32, eps):
    return lax.rsqrt(jnp.mean(x32 * x32, axis=-1, keepdims=True) + eps)


def _layer_spec(l, shape, col=None):
    if col is None:
        return pl.BlockSpec((None,) + shape, lambda *g: (l, 0, 0))
    return pl.BlockSpec((None,) + shape, lambda *g: (l, 0, col(*g)))


def _rope_table_kernel(pos_ref, invf_ref, m1_ref, m2_ref, c_ref, s1_ref, s2_ref):
    ang = pos_ref[...].astype(jnp.float32) * invf_ref[...]
    sn = jnp.sin(ang)
    c_ref[...] = jnp.cos(ang)
    s1_ref[...] = sn * m1_ref[...]
    s2_ref[...] = sn * m2_ref[...]


def _rope_tables(pos_col, invf, m1, m2):
    m = pos_col.shape[0]
    row = pl.BlockSpec((1, LANES), lambda i: (0, 0))
    tab = pl.BlockSpec((TM_PREP, LANES), lambda i: (i, 0))
    return pl.pallas_call(
        _rope_table_kernel,
        grid=(m // TM_PREP,),
        in_specs=[pl.BlockSpec((TM_PREP, 1), lambda i: (i, 0)), row, row, row],
        out_specs=[tab, tab, tab],
        out_shape=[jax.ShapeDtypeStruct((m, LANES), jnp.float32)] * 3,
        compiler_params=_cparams(("parallel",)),
        name="rope_tables",
    )(pos_col, invf, m1, m2)


def _rope_lane_patterns():
    lane = np.arange(LANES)
    half_a = PARTIAL_ROT_DIM // 2
    r = lane % DIFF_QK_DIM
    inv_a = ROPE_THETA ** (-jnp.arange(0, PARTIAL_ROT_DIM, 2, dtype=jnp.float32) / PARTIAL_ROT_DIM)
    invf_a = jnp.where(jnp.asarray(r < PARTIAL_ROT_DIM), inv_a[jnp.asarray(r % half_a)], 0.0)
    m1_a = np.where(r < half_a, -1.0, 0.0)
    m2_a = np.where((r >= half_a) & (r < PARTIAL_ROT_DIM), 1.0, 0.0)
    half_b = MLA_QK_ROPE // 2
    inv_b = ROPE_THETA ** (-jnp.arange(0, MLA_QK_ROPE, 2, dtype=jnp.float32) / MLA_QK_ROPE)
    invf_b = jnp.where(jnp.asarray(lane < MLA_QK_ROPE), inv_b[jnp.asarray(lane % half_b)], 0.0)
    m1_b = np.where(lane < half_b, -1.0, 0.0)
    m2_b = np.where((lane >= half_b) & (lane < MLA_QK_ROPE), 1.0, 0.0)
    f = lambda a: jnp.asarray(a, jnp.float32).reshape(1, LANES)
    return (f(invf_a), f(m1_a), f(m2_a)), (f(invf_b), f(m1_b), f(m2_b))


def _rope_block(x, c, s1, s2, d):
    return x * c + pltpu.roll(x, LANES - d, 1) * s1 + pltpu.roll(x, d, 1) * s2


def _in_proj_kernel(x_ref, g_ref, w_ref, c_ref, s1_ref, s2_ref, z_ref, zg_ref, h_ref):
    j = pl.program_id(1)

    @pl.when(j == 0)
    def _():
        x = x_ref[...]
        h_ref[...] = (x * _rms_scale(x, NORM_EPS) * g_ref[...]).astype(jnp.bfloat16)

    @pl.when(j == 0)
    def _():
        c, s1, s2 = c_ref[...], s1_ref[...], s2_ref[...]
        scale = DIFF_QK_DIM ** -0.5 * LOG2E
        acc = jnp.dot(h_ref[...], w_ref[...], preferred_element_type=jnp.float32)
        for b in range(TN_IN // LANES):
            sl = slice(b * LANES, (b + 1) * LANES)
            r = _rope_block(acc[:, sl], c, s1, s2, PARTIAL_ROT_DIM // 2)
            z_ref[:, sl] = ((r * scale) if b * LANES < Z_AK else r).astype(jnp.bfloat16)

    @pl.when(j != 0)
    def _():
        acc = jnp.dot(h_ref[...], w_ref[...], preferred_element_type=jnp.float32)
        z_ref[...] = acc.astype(jnp.bfloat16)

        @pl.when(j == Z_FG // TN_IN)
        def _():
            off = Z_FG % TN_IN // LANES * LANES
            zg_ref[...] = acc[:, off:off + LANES]


def _in_proj(l, x, gain, w, tabs):
    m, d = x.shape
    assert Z_AQ == 0 and TN_IN == Z_AV
    tab = pl.BlockSpec((TM_PROJ, LANES), lambda i, j: (i, 0))
    return pl.pallas_call(
        _in_proj_kernel,
        grid=(m // TM_PROJ, Z_WIDTH // TN_IN),
        in_specs=[pl.BlockSpec((TM_PROJ, d), lambda i, j: (i, 0)),
                  _layer_spec(l, (1, d)),
                  _layer_spec(l, (d, TN_IN), lambda i, j: j),
                  tab, tab, tab],
        out_specs=[pl.BlockSpec((TM_PROJ, TN_IN), lambda i, j: (i, j)),
                   pl.BlockSpec((TM_PROJ, LANES), lambda i, j: (i, 0))],
        out_shape=[jax.ShapeDtypeStruct((m, Z_WIDTH), jnp.bfloat16),
                   jax.ShapeDtypeStruct((m, LANES), jnp.float32)],
        scratch_shapes=[pltpu.VMEM((TM_PROJ, d), jnp.bfloat16)],
        compiler_params=_cparams(("parallel", "arbitrary")),
        name="in_proj",
    )(x, gain, w, *tabs)


def _mla_up_kernel(cq_ref, ckv_ref, kr_ref, gq_ref, gkv_ref, wq_ref, wk_ref, wv_ref,
                   c_ref, s1_ref, s2_ref, q_ref, k_ref, v_ref):
    c, s1, s2 = c_ref[...], s1_ref[...], s2_ref[...]
    half = MLA_QK_ROPE // 2
    scale = (MLA_QK_NOPE + MLA_QK_ROPE) ** -0.5 * LOG2E

    cq = cq_ref[...].astype(jnp.float32)
    cqn = (cq * _rms_scale(cq, NORM_EPS) * gq_ref[...]).astype(jnp.bfloat16)
    q = jnp.dot(cqn, wq_ref[...], preferred_element_type=jnp.float32)
    ckv = ckv_ref[...].astype(jnp.float32)
    ckvn = (ckv * _rms_scale(ckv, NORM_EPS) * gkv_ref[...]).astype(jnp.bfloat16)
    kn = jnp.dot(ckvn, wk_ref[...], preferred_element_type=jnp.float32)
    v_ref[...] = jnp.dot(ckvn, wv_ref[...], preferred_element_type=jnp.float32).astype(jnp.bfloat16)
    kr = _rope_block(kr_ref[...].astype(jnp.float32), c, s1, s2, half)
    lane = lax.broadcasted_iota(jnp.int32, kr.shape, 1)
    kr = jnp.where(lane < MLA_QK_ROPE, kr, 0.0).astype(jnp.bfloat16)

    for h in range(MLA_HEADS):
        base = h * MLA_QK_PAD
        q_ref[:, base:base + LANES] = (q[:, base:base + LANES] * scale).astype(jnp.bfloat16)
        qr = _rope_block(q[:, base + LANES:base + 2 * LANES], c, s1, s2, half)
        q_ref[:, base + LANES:base + 2 * LANES] = (qr * scale).astype(jnp.bfloat16)
        k_ref[:, base:base + LANES] = kn[:, h * LANES:(h + 1) * LANES].astype(jnp.bfloat16)
        k_ref[:, base + LANES:base + 2 * LANES] = kr


def _mla_up(l, z, gq, gkv, wq, wk, wv, tabs):
    m = z.shape[0]
    hq = MLA_HEADS * MLA_QK_PAD
    hv = MLA_HEADS * HEAD_DIM
    tab = pl.BlockSpec((TM_PREP, LANES), lambda i: (i, 0))
    return pl.pallas_call(
        _mla_up_kernel,
        grid=(m // TM_PREP,),
        in_specs=[pl.BlockSpec((TM_PREP, MLA_LORA), lambda i: (i, Z_CQ // MLA_LORA)),
                  pl.BlockSpec((TM_PREP, MLA_LORA), lambda i: (i, Z_CKV // MLA_LORA)),
                  pl.BlockSpec((TM_PREP, LANES), lambda i: (i, Z_KR // LANES)),
                  _layer_spec(l, (1, MLA_LORA)), _layer_spec(l, (1, MLA_LORA)),
                  _layer_spec(l, (MLA_LORA, hq)), _layer_spec(l, (MLA_LORA, hv)),
                  _layer_spec(l, (MLA_LORA, hv)),
                  tab, tab, tab],
        out_specs=[pl.BlockSpec((TM_PREP, hq), lambda i: (i, 0)),
                   pl.BlockSpec((TM_PREP, hq), lambda i: (i, 0)),
                   pl.BlockSpec((TM_PREP, hv), lambda i: (i, 0))],
        out_shape=[jax.ShapeDtypeStruct((m, hq), jnp.bfloat16),
                   jax.ShapeDtypeStruct((m, hq), jnp.bfloat16),
                   jax.ShapeDtypeStruct((m, hv), jnp.bfloat16)],
        compiler_params=_cparams(("parallel",)),
        name="mla_up",
    )(z, z, z, gq, gkv, wq, wk, wv, *tabs)


def _fox_gate_kernel(zg_ref, fb_ref, col_ref, row_ref):
    x = zg_ref[...] + fb_ref[...]
    logf = jnp.minimum(x, 0.0) - jnp.log(1.0 + jnp.exp(-jnp.abs(x)))
    s = logf.shape[0]
    row = lax.broadcasted_iota(jnp.int32, logf.shape, 0)
    d = 1
    while d < s:
        logf = logf + jnp.where(row >= d, pltpu.roll(logf, d, 0), 0.0)
        d *= 2
    logf = logf * LOG2E
    col_ref[...] = logf
    row_ref[0] = logf.T[FOX_SHIFT:FOX_SHIFT + SUBLANES, :]


def _fox_gate(l, zg, fb, batch, seq):
    return pl.pallas_call(
        _fox_gate_kernel,
        grid=(batch,),
        in_specs=[pl.BlockSpec((seq, LANES), lambda b: (b, 0)),
                  _layer_spec(l, (1, LANES))],
        out_specs=[pl.BlockSpec((seq, LANES), lambda b: (b, 0)),
                   pl.BlockSpec((1, SUBLANES, seq), lambda b: (b, 0, 0))],
        out_shape=[jax.ShapeDtypeStruct((batch * seq, LANES), jnp.float32),
                   jax.ShapeDtypeStruct((batch, SUBLANES, seq), jnp.float32)],
        compiler_params=_cparams(("parallel",)),
        name="fox_gate",
    )(zg, fb)


_NT = (((1,), (1,)), ((), ()))


def _scores(q, k_ref, s_ref, bias_fn=None):
    n = s_ref.shape[0] // TQ
    for c in range(n):
        cs = slice(c * TQ, (c + 1) * TQ)
        s = lax.dot_general(q[c * TQ:, :], k_ref[cs, :], _NT, preferred_element_type=jnp.float32)
        if bias_fn is not None:
            s = s + bias_fn(c)
        s_ref[c * TQ:, cs] = s


def _softmax_tile(s_ref, p_ref, i):
    rs = slice(i * TQ, (i + 1) * TQ)

    def chunk(c):
        s = s_ref[rs, c * TQ:(c + 1) * TQ]
        if c == i:
            rows = lax.broadcasted_iota(jnp.int32, s.shape, 0)
            cols = lax.broadcasted_iota(jnp.int32, s.shape, 1)
            s = jnp.where(cols <= rows, s, -jnp.inf)
        return s

    mpart = None
    for c in range(i + 1):
        s = chunk(c)
        for b in range(TQ // LANES):
            blk = s[:, b * LANES:(b + 1) * LANES]
            mpart = blk if mpart is None else jnp.maximum(mpart, blk)
    m = jnp.max(mpart, axis=1, keepdims=True)
    for c in range(i + 1):
        p_ref[rs, c * TQ:(c + 1) * TQ] = jnp.exp2(chunk(c) - m).astype(jnp.bfloat16)


def _fill_values(va_ref, v):
    va_ref[:, :HEAD_DIM] = v
    va_ref[:, HEAD_DIM:] = jnp.ones((va_ref.shape[0], va_ref.shape[1] - HEAD_DIM), va_ref.dtype)


def _attend(p_ref, va_ref, i):
    n = (i + 1) * TQ
    o = jnp.dot(p_ref[i * TQ:(i + 1) * TQ, :n], va_ref[:n, :], preferred_element_type=jnp.float32)
    return o[:, :HEAD_DIM] / o[:, HEAD_DIM:]


def _mla_attn_kernel(q_ref, k_ref, v_ref, o_ref, s_ref, p_ref, va_ref):
    _fill_values(va_ref, v_ref[...])
    _scores(q_ref[...], k_ref, s_ref)
    for i in range(q_ref.shape[0] // TQ):
        _softmax_tile(s_ref, p_ref, i)
        o_ref[i * TQ:(i + 1) * TQ, :] = _attend(p_ref, va_ref, i).astype(o_ref.dtype)


def _unshift(lo, hi):
    return jnp.concatenate([lo[:, FOX_SHIFT:], hi[:, :FOX_SHIFT]], axis=1)


def _fox_attn_kernel(qlo_ref, qhi_ref, klo_ref, khi_ref, vlo_ref, vhi_ref, ccol_ref, crow_ref,
                     o_ref, s_ref, p_ref, va_ref, k_ref):
    h = pl.program_id(1)
    scale = HEAD_DIM ** -0.5 * LOG2E
    k_ref[...] = _unshift(klo_ref[...], khi_ref[...])
    _fill_values(va_ref, _unshift(vlo_ref[...], vhi_ref[...]))
    q = _unshift(qlo_ref[...], qhi_ref[...])
    q = (q.astype(jnp.float32) * scale).astype(jnp.bfloat16)
    crow = crow_ref[0, pl.ds(h, 1), :]
    cc = ccol_ref[...]
    lane = lax.broadcasted_iota(jnp.int32, cc.shape, 1)
    ccq = jnp.sum(jnp.where(lane == h + FOX_SHIFT, cc, 0.0), axis=1, keepdims=True)
    _scores(q, k_ref, s_ref, bias_fn=lambda c: ccq[c * TQ:, :] - crow[:, c * TQ:(c + 1) * TQ])
    for i in range(o_ref.shape[0] // TQ):
        _softmax_tile(s_ref, p_ref, i)
        o_ref[i * TQ:(i + 1) * TQ, :] = _attend(p_ref, va_ref, i).astype(o_ref.dtype)


def _diff_attn_kernel(lambda_init, q_ref, k_ref, v_ref, lam_ref, g_ref, o_ref, s_ref, p_ref, va_ref, o1_ref):
    lv = lam_ref[...]
    lam = (jnp.exp(jnp.sum(lv[0:1] * lv[1:2], axis=1, keepdims=True))
           - jnp.exp(jnp.sum(lv[2:3] * lv[3:4], axis=1, keepdims=True)) + lambda_init)
    gain = g_ref[...] * (1.0 - lambda_init)
    q = q_ref[...]
    lane = lax.broadcasted_iota(jnp.int32, q.shape, 1)
    zero = jnp.zeros_like(q)
    nq = q_ref.shape[0] // TQ
    _fill_values(va_ref, v_ref[...])
    _scores(jnp.where(lane < DIFF_QK_DIM, q, zero), k_ref, s_ref)
    for i in range(nq):
        _softmax_tile(s_ref, p_ref, i)
        o1_ref[i * TQ:(i + 1) * TQ, :] = _attend(p_ref, va_ref, i)
    _scores(jnp.where(lane >= DIFF_QK_DIM, q, zero), k_ref, s_ref)
    for i in range(nq):
        rs = slice(i * TQ, (i + 1) * TQ)
        _softmax_tile(s_ref, p_ref, i)
        o = o1_ref[rs, :] - lam * _attend(p_ref, va_ref, i)
        o_ref[rs, :] = (o * _rms_scale(o, SUBLN_EPS) * gain).astype(o_ref.dtype)


def _attn_call(kernel, name, batch, seq, heads, ins, in_specs, extra_scratch=()):
    return pl.pallas_call(
        kernel,
        grid=(batch, heads),
        in_specs=in_specs,
        out_specs=pl.BlockSpec((seq, HEAD_DIM), lambda b, h: (b, h)),
        out_shape=jax.ShapeDtypeStruct((batch * seq, heads * HEAD_DIM), jnp.bfloat16),
        scratch_shapes=[pltpu.VMEM((seq, seq), jnp.float32), pltpu.VMEM((seq, seq), jnp.bfloat16),
                        pltpu.VMEM((seq, 2 * HEAD_DIM), jnp.bfloat16), *extra_scratch],
        compiler_params=_cparams(("parallel", "parallel")),
        name=name,
    )(*ins)


def _colblk(seq, width, col0):
    assert col0 % width == 0
    return pl.BlockSpec((seq, width), lambda b, h: (b, col0 // width + h))


def _diff_attn(l, z, lam, gain, lambda_init, batch, seq):
    kern = functools.partial(_diff_attn_kernel, lambda_init)
    specs = [_colblk(seq, HEAD_DIM, Z_AQ), _colblk(seq, HEAD_DIM, Z_AK), _colblk(seq, HEAD_DIM, Z_AV),
             _layer_spec(l, (4, DIFF_QK_DIM)), _layer_spec(l, (1, HEAD_DIM))]
    return _attn_call(kern, "diff_attn", batch, seq, DIFF_HEADS, (z, z, z, lam, gain), specs,
                      extra_scratch=(pltpu.VMEM((seq, HEAD_DIM), jnp.float32),))


def _mla_attn(q, k, v, batch, seq):
    specs = [_colblk(seq, MLA_QK_PAD, 0), _colblk(seq, MLA_QK_PAD, 0), _colblk(seq, HEAD_DIM, 0)]
    return _attn_call(_mla_attn_kernel, "mla_attn", batch, seq, MLA_HEADS, (q, k, v), specs)


def _fox_attn(z, ccol, crow, batch, seq):
    specs = []
    for col0 in (Z_FQ, Z_FK, Z_FV):
        specs += [_colblk(seq, LANES, col0 - FOX_SHIFT), _colblk(seq, LANES, col0 - FOX_SHIFT + LANES)]
    specs += [pl.BlockSpec((seq, LANES), lambda b, h: (b, 0)),
              pl.BlockSpec((1, SUBLANES, seq), lambda b, h: (b, 0, 0))]
    kv = pltpu.VMEM((seq, HEAD_DIM), jnp.bfloat16)
    return _attn_call(_fox_attn_kernel, "fox_attn", batch, seq, FOX_HEADS, (z,) * 6 + (ccol, crow), specs,
                      extra_scratch=(kv,))


def _out_proj_kernel(x_ref, a_ref, b_ref, c_ref, wa_ref, wb_ref, wc_ref, o_ref):
    acc = jnp.dot(a_ref[...], wa_ref[...], preferred_element_type=jnp.float32)
    acc = acc + jnp.dot(b_ref[...], wb_ref[...], preferred_element_type=jnp.float32)
    acc = acc + jnp.dot(c_ref[...], wc_ref[...], preferred_element_type=jnp.float32)
    o_ref[...] = x_ref[...] + acc


def _out_proj(l, x, oa, ob, oc, wa, wb, wc):
    m, d = x.shape
    lhs = lambda a: pl.BlockSpec((TM_OUT, a.shape[1]), lambda i, j: (i, 0))
    rhs = lambda w: _layer_spec(l, (w.shape[1], TN_OUT), lambda i, j: j)
    xo = pl.BlockSpec((TM_OUT, TN_OUT), lambda i, j: (i, j))
    return pl.pallas_call(
        _out_proj_kernel,
        grid=(m // TM_OUT, d // TN_OUT),
        in_specs=[xo, lhs(oa), lhs(ob), lhs(oc), rhs(wa), rhs(wb), rhs(wc)],
        out_specs=xo,
        out_shape=jax.ShapeDtypeStruct((m, d), jnp.float32),
        compiler_params=_cparams(("parallel", "parallel")),
        name="out_proj",
    )(x, oa, ob, oc, wa, wb, wc)


def _ffn_up_kernel(tiles_per_seq, x_ref, g_ref, wa_ref, wg_ref, cwa_ref, cwg_ref, cba_ref, cbg_ref,
                   o_ref, h_ref, ua_ref, ug_ref, carry_ref):
    i, j = pl.program_id(0), pl.program_id(1)
    tm = x_ref.shape[0]

    @pl.when(j == 0)
    def _():
        x = x_ref[...]
        h_ref[...] = (x * _rms_scale(x, NORM_EPS) * g_ref[...]).astype(jnp.bfloat16)

    @pl.when(i % tiles_per_seq == 0)
    def _():
        carry_ref[j] = jnp.zeros(carry_ref.shape[1:], jnp.float32)

    ua_ref[0:SUBLANES, :] = carry_ref[j, 0]
    ug_ref[0:SUBLANES, :] = carry_ref[j, 1]

    def conv(u_ref, w_ref, half, cw_ref, cb_ref):
        u = jnp.dot(h_ref[...], w_ref[...].astype(jnp.bfloat16), preferred_element_type=jnp.float32)
        u_ref[SUBLANES:, :] = u
        carry_ref[j, half] = u[tm - SUBLANES:, :]
        cw = cw_ref[...]
        return (u * cw[2:3] + u_ref[SUBLANES - 1:SUBLANES - 1 + tm, :] * cw[1:2]
                + u_ref[SUBLANES - 2:SUBLANES - 2 + tm, :] * cw[0:1] + cb_ref[...])

    g = conv(ug_ref, wg_ref, 1, cwg_ref, cbg_ref)
    gate = g / (1.0 + jnp.exp(-g))
    a = conv(ua_ref, wa_ref, 0, cwa_ref, cba_ref)
    o_ref[...] = (gate * a).astype(o_ref.dtype)


def _ffn_up(l, x, gain, w_up, conv_w, conv_b, seq):
    m, d = x.shape
    nj = D_FF // TN_FF
    kern = functools.partial(_ffn_up_kernel, seq // TM_PROJ)
    col = lambda rows, off: _layer_spec(l, (rows, TN_FF), lambda i, j: j + off)
    return pl.pallas_call(
        kern,
        grid=(m // TM_PROJ, nj),
        in_specs=[pl.BlockSpec((TM_PROJ, d), lambda i, j: (i, 0)),
                  _layer_spec(l, (1, d)),
                  col(d, 0), col(d, nj), col(3, 0), col(3, nj), col(1, 0), col(1, nj)],
        out_specs=pl.BlockSpec((TM_PROJ, TN_FF), lambda i, j: (i, j)),
        out_shape=jax.ShapeDtypeStruct((m, D_FF), jnp.bfloat16),
        scratch_shapes=[pltpu.VMEM((TM_PROJ, d), jnp.bfloat16),
                        pltpu.VMEM((TM_PROJ + SUBLANES, TN_FF), jnp.float32),
                        pltpu.VMEM((TM_PROJ + SUBLANES, TN_FF), jnp.float32),
                        pltpu.VMEM((nj, 2, SUBLANES, TN_FF), jnp.float32)],
        compiler_params=_cparams(("arbitrary", "arbitrary")),
        name="ffn_up",
    )(x, gain, w_up, w_up, conv_w, conv_w, conv_b, conv_b)


def _ffn_down_kernel(x_ref, a_ref, w_ref, o_ref):
    o_ref[...] = x_ref[...] + jnp.dot(a_ref[...], w_ref[...], preferred_element_type=jnp.float32)


def _ffn_down(l, x, act, w):
    m, d = x.shape
    k = act.shape[1]
    xo = pl.BlockSpec((TM_PROJ, TN_DOWN), lambda i, j: (i, j))
    return pl.pallas_call(
        _ffn_down_kernel,
        grid=(m // TM_PROJ, d // TN_DOWN),
        in_specs=[xo, pl.BlockSpec((TM_PROJ, k), lambda i, j: (i, 0)),
                  _layer_spec(l, (k, TN_DOWN), lambda i, j: j)],
        out_specs=xo,
        out_shape=jax.ShapeDtypeStruct((m, d), jnp.float32),
        compiler_params=_cparams(("parallel", "parallel")),
        name="ffn_down",
    )(x, act, w)


def _final_norm_kernel(x_ref, g_ref, o_ref):
    x = x_ref[...]
    o_ref[...] = x * _rms_scale(x, NORM_EPS) * g_ref[...]


def _final_norm(x, gain):
    m, d = x.shape
    blk = pl.BlockSpec((TM_PREP, d), lambda i: (i, 0))
    return pl.pallas_call(
        _final_norm_kernel,
        grid=(m // TM_PREP,),
        in_specs=[blk, pl.BlockSpec((1, d), lambda i: (0, 0))],
        out_specs=blk,
        out_shape=jax.ShapeDtypeStruct((m, d), jnp.float32),
        compiler_params=_cparams(("parallel",)),
        name="final_norm",
    )(x, gain)


def _w_in_prep_kernel(w_ref, t_ref, o_ref):
    c = pl.program_id(0)
    n_main = pl.num_programs(0) - t_ref.shape[2] // LANES

    @pl.when(c < n_main)
    def _():
        for l in range(o_ref.shape[0]):
            o_ref[l] = w_ref[:, l, :].T.astype(jnp.bfloat16)

    for t in range(t_ref.shape[2] // LANES):
        @pl.when(c == n_main + t)
        def _():
            o_ref[...] = t_ref[:, :, t * LANES:(t + 1) * LANES]


def _w_in_prep(w, width):
    dp, k, n = w.shape
    n_main = n // LANES
    tail_w = width - n_main * LANES
    assert width % LANES == 0 and tail_w >= n - n_main * LANES
    wt = jnp.transpose(w, (2, 0, 1))
    tail = jnp.pad(w[:, :, n_main * LANES:], ((0, 0), (0, 0), (0, tail_w - (n - n_main * LANES))))
    return pl.pallas_call(
        _w_in_prep_kernel,
        grid=(width // LANES,),
        in_specs=[pl.BlockSpec((LANES, dp, k), lambda c: (jnp.minimum(c, n_main - 1), 0, 0)),
                  pl.BlockSpec((dp, k, tail_w), lambda c: (0, 0, 0))],
        out_specs=pl.BlockSpec((dp, k, LANES), lambda c: (0, 0, c)),
        out_shape=jax.ShapeDtypeStruct((dp, k, width), jnp.bfloat16),
        compiler_params=_cparams(("arbitrary",)),
        name="w_in_prep",
    )(wt, tail.astype(jnp.bfloat16))


def _pad_w_uq(w):
    dp, r, _ = w.shape
    w = w.reshape(dp, r, MLA_HEADS, MLA_QK_NOPE + MLA_QK_ROPE)
    w = jnp.pad(w, ((0, 0), (0, 0), (0, 0), (0, MLA_QK_PAD - MLA_QK_NOPE - MLA_QK_ROPE)))
    return w.reshape(dp, r, MLA_HEADS * MLA_QK_PAD).astype(jnp.bfloat16)


def _split_w_ukv(w):
    dp, r, _ = w.shape
    w = w.reshape(dp, r, MLA_HEADS, MLA_QK_NOPE + HEAD_DIM)
    wk = w[..., :MLA_QK_NOPE].reshape(dp, r, MLA_HEADS * MLA_QK_NOPE)
    wv = w[..., MLA_QK_NOPE:].reshape(dp, r, MLA_HEADS * HEAD_DIM)
    return wk.astype(jnp.bfloat16), wv.astype(jnp.bfloat16)


def kernel(x, positions, attn_norm, w_in, diff_lambda, diff_out_norm, mla_q_norm, mla_kv_norm, mla_w_uq,
           mla_w_ukv, fox_forget_bias, w_o, ffn_norm, ffn_w_up, ffn_conv_w, ffn_conv_b, ffn_w_down,
           final_norm):
    batch, seq, d = x.shape
    depth = w_in.shape[0]
    m = batch * seq
    assert seq % TQ == 0 and seq % TM_PROJ == 0 and m % TM_PROJ == 0
    assert w_in.shape[2] == IN_WIDTH

    pat_a, pat_b = _rope_lane_patterns()
    pos_col = positions.reshape(m, 1)
    tabs_a = _rope_tables(pos_col, *pat_a)
    tabs_b = _rope_tables(pos_col, *pat_b)

    bf = lambda a: a.astype(jnp.bfloat16)
    rows = lambda a: a.reshape(depth, 1, -1)
    n_a, n_b = DIFF_HEADS * HEAD_DIM, MLA_HEADS * HEAD_DIM
    w_in_b = _w_in_prep(w_in, Z_WIDTH)
    w_uq_b = _pad_w_uq(mla_w_uq)
    w_uk_b, w_uv_b = _split_w_ukv(mla_w_ukv)
    wo_a, wo_b, wo_c = bf(w_o[:, :n_a]), bf(w_o[:, n_a:n_a + n_b]), bf(w_o[:, n_a + n_b:])
    w_down_b = bf(ffn_w_down)
    fb = jnp.pad(fox_forget_bias, ((0, 0), (FOX_SHIFT, LANES - FOX_SHIFT - FOX_HEADS))).reshape(depth, 1, LANES)
    g_attn, g_ffn, g_q, g_kv, g_diff = (rows(attn_norm), rows(ffn_norm), rows(mla_q_norm),
                                        rows(mla_kv_norm), rows(diff_out_norm))
    conv_b = rows(ffn_conv_b)

    xf = x.reshape(m, d)
    for l in range(depth):
        lambda_init = 0.8 - 0.6 * math.exp(-0.3 * l)
        z, zg = _in_proj(l, xf, g_attn, w_in_b, tabs_a)
        q_b, k_b, v_b = _mla_up(l, z, g_q, g_kv, w_uq_b, w_uk_b, w_uv_b, tabs_b)
        ccol, crow = _fox_gate(l, zg, fb, batch, seq)
        o_a = _diff_attn(l, z, diff_lambda, g_diff, lambda_init, batch, seq)
        o_b = _mla_attn(q_b, k_b, v_b, batch, seq)
        o_c = _fox_attn(z, ccol, crow, batch, seq)
        xf = _out_proj(l, xf, o_a, o_b, o_c, wo_a, wo_b, wo_c)
        act = _ffn_up(l, xf, g_ffn, ffn_w_up, ffn_conv_w, conv_b, seq)
        xf = _ffn_down(l, xf, act, w_down_b)
    return _final_norm(xf, final_norm.reshape(1, d)).reshape(batch, seq, d)
```

```python
import functools
import math

import numpy as np
import jax
import jax.numpy as jnp
from jax import lax
from jax.experimental import pallas as pl
from jax.experimental.pallas import tpu as pltpu

HEAD_DIM = 128
DIFF_HEADS = 4
DIFF_QK_DIM = 64
MLA_HEADS = 6
MLA_LORA = 512
MLA_QK_NOPE = 128
MLA_QK_ROPE = 64
FOX_HEADS = 6
D_FF = 5632
ROPE_THETA = 500000.0
PARTIAL_ROT_DIM = DIFF_QK_DIM // 4
NORM_EPS = 1e-6
SUBLN_EPS = 1e-5
LOG2E = math.log2(math.e)

LANES = 128
SUBLANES = 8
VMEM_LIMIT_BYTES = 56 * 1024 * 1024

Z_AQ, Z_AK, Z_AV = 0, 512, 1024
Z_CQ, Z_CKV, Z_KR = 1536, 2048, 2560
Z_FQ, Z_FK, Z_FV = 2624, 3392, 4160
Z_FG = 4928
IN_WIDTH = Z_FG + FOX_HEADS
FOX_SHIFT = Z_FQ % LANES
assert Z_FK % LANES == FOX_SHIFT and Z_FV % LANES == FOX_SHIFT and Z_FG % LANES == FOX_SHIFT
MLA_QK_PAD = 256

TM_PROJ = 1024
TN_IN = 1280
TM_OUT = 512
TN_FF = 512
TN_DOWN = 512
TM_PREP = 512
TQ = 256
HEADS_PER_STEP = 1
Z_WIDTH = -(-IN_WIDTH // TN_IN) * TN_IN


def _cparams(sem):
    return pltpu.CompilerParams(dimension_semantics=sem, vmem_limit_bytes=VMEM_LIMIT_BYTES)


def _rms_scale(x32, eps):
    return lax.rsqrt(jnp.mean(x32 * x32, axis=-1, keepdims=True) + eps)


def _layer_spec(l, shape, col=None):
    if col is None:
        return pl.BlockSpec((None,) + shape, lambda *g: (l, 0, 0))
    return pl.BlockSpec((None,) + shape, lambda *g: (l, 0, col(*g)))


def _rope_table_kernel(pos_ref, invf_ref, m1_ref, m2_ref, c_ref, s1_ref, s2_ref):
    ang = pos_ref[...].astype(jnp.float32) * invf_ref[...]
    sn = jnp.sin(ang)
    c_ref[...] = jnp.cos(ang)
    s1_ref[...] = sn * m1_ref[...]
    s2_ref[...] = sn * m2_ref[...]


def _rope_tables(pos_col, invf, m1, m2):
    m = pos_col.shape[0]
    row = pl.BlockSpec((1, LANES), lambda i: (0, 0))
    tab = pl.BlockSpec((TM_PREP, LANES), lambda i: (i, 0))
    return pl.pallas_call(
        _rope_table_kernel,
        grid=(m // TM_PREP,),
        in_specs=[pl.BlockSpec((TM_PREP, 1), lambda i: (i, 0)), row, row, row],
        out_specs=[tab, tab, tab],
        out_shape=[jax.ShapeDtypeStruct((m, LANES), jnp.float32)] * 3,
        compiler_params=_cparams(("parallel",)),
        name="rope_tables",
    )(pos_col, invf, m1, m2)


def _rope_lane_patterns():
    lane = np.arange(LANES)
    half_a = PARTIAL_ROT_DIM // 2
    r = lane % DIFF_QK_DIM
    inv_a = ROPE_THETA ** (-jnp.arange(0, PARTIAL_ROT_DIM, 2, dtype=jnp.float32) / PARTIAL_ROT_DIM)
    invf_a = jnp.where(jnp.asarray(r < PARTIAL_ROT_DIM), inv_a[jnp.asarray(r % half_a)], 0.0)
    m1_a = np.where(r < half_a, -1.0, 0.0)
    m2_a = np.where((r >= half_a) & (r < PARTIAL_ROT_DIM), 1.0, 0.0)
    half_b = MLA_QK_ROPE // 2
    inv_b = ROPE_THETA ** (-jnp.arange(0, MLA_QK_ROPE, 2, dtype=jnp.float32) / MLA_QK_ROPE)
    invf_b = jnp.where(jnp.asarray(lane < MLA_QK_ROPE), inv_b[jnp.asarray(lane % half_b)], 0.0)
    m1_b = np.where(lane < half_b, -1.0, 0.0)
    m2_b = np.where((lane >= half_b) & (lane < MLA_QK_ROPE), 1.0, 0.0)
    f = lambda a: jnp.asarray(a, jnp.float32).reshape(1, LANES)
    return (f(invf_a), f(m1_a), f(m2_a)), (f(invf_b), f(m1_b), f(m2_b))


def _rope_block(x, c, s1, s2, d):
    return x * c + pltpu.roll(x, LANES - d, 1) * s1 + pltpu.roll(x, d, 1) * s2


def _in_proj_kernel(x_ref, g_ref, w_ref, c_ref, s1_ref, s2_ref, z_ref, zg_ref, h_ref):
    j = pl.program_id(1)

    @pl.when(j == 0)
    def _():
        x = x_ref[...]
        h_ref[...] = (x * _rms_scale(x, NORM_EPS) * g_ref[...]).astype(jnp.bfloat16)

    @pl.when(j == 0)
    def _():
        c, s1, s2 = c_ref[...], s1_ref[...], s2_ref[...]
        scale = DIFF_QK_DIM ** -0.5 * LOG2E
        acc = jnp.dot(h_ref[...], w_ref[...], preferred_element_type=jnp.float32)
        for b in range(Z_AV // LANES):
            sl = slice(b * LANES, (b + 1) * LANES)
            r = _rope_block(acc[:, sl], c, s1, s2, PARTIAL_ROT_DIM // 2)
            z_ref[:, sl] = ((r * scale) if b * LANES < Z_AK else r).astype(jnp.bfloat16)
        z_ref[:, Z_AV:] = acc[:, Z_AV:].astype(jnp.bfloat16)

    @pl.when(j != 0)
    def _():
        acc = jnp.dot(h_ref[...], w_ref[...], preferred_element_type=jnp.float32)
        z_ref[...] = acc.astype(jnp.bfloat16)

        @pl.when(j == Z_FG // TN_IN)
        def _():
            off = Z_FG % TN_IN // LANES * LANES
            zg_ref[...] = acc[:, off:off + LANES]


def _in_proj(l, x, gain, w, tabs):
    m, d = x.shape
    assert Z_AQ == 0 and TN_IN >= Z_AV
    tab = pl.BlockSpec((TM_PROJ, LANES), lambda i, j: (i, 0))
    return pl.pallas_call(
        _in_proj_kernel,
        grid=(m // TM_PROJ, Z_WIDTH // TN_IN),
        in_specs=[pl.BlockSpec((TM_PROJ, d), lambda i, j: (i, 0)),
                  _layer_spec(l, (1, d)),
                  _layer_spec(l, (d, TN_IN), lambda i, j: j),
                  tab, tab, tab],
        out_specs=[pl.BlockSpec((TM_PROJ, TN_IN), lambda i, j: (i, j)),
                   pl.BlockSpec((TM_PROJ, LANES), lambda i, j: (i, 0))],
        out_shape=[jax.ShapeDtypeStruct((m, Z_WIDTH), jnp.bfloat16),
                   jax.ShapeDtypeStruct((m, LANES), jnp.float32)],
        scratch_shapes=[pltpu.VMEM((TM_PROJ, d), jnp.bfloat16)],
        compiler_params=_cparams(("parallel", "arbitrary")),
        name="in_proj",
    )(x, gain, w, *tabs)


def _mla_up_kernel(cq_ref, ckv_ref, kr_ref, gq_ref, gkv_ref, wq_ref, wk_ref, wv_ref,
                   c_ref, s1_ref, s2_ref, q_ref, k_ref, v_ref):
    c, s1, s2 = c_ref[...], s1_ref[...], s2_ref[...]
    half = MLA_QK_ROPE // 2
    scale = (MLA_QK_NOPE + MLA_QK_ROPE) ** -0.5 * LOG2E

    cq = cq_ref[...].astype(jnp.float32)
    cqn = (cq * _rms_scale(cq, NORM_EPS) * gq_ref[...]).astype(jnp.bfloat16)
    q = jnp.dot(cqn, wq_ref[...], preferred_element_type=jnp.float32)
    ckv = ckv_ref[...].astype(jnp.float32)
    ckvn = (ckv * _rms_scale(ckv, NORM_EPS) * gkv_ref[...]).astype(jnp.bfloat16)
    kn = jnp.dot(ckvn, wk_ref[...], preferred_element_type=jnp.float32)
    v_ref[...] = jnp.dot(ckvn, wv_ref[...], preferred_element_type=jnp.float32).astype(jnp.bfloat16)
    kr = _rope_block(kr_ref[...].astype(jnp.float32), c, s1, s2, half)
    lane = lax.broadcasted_iota(jnp.int32, kr.shape, 1)
    kr = jnp.where(lane < MLA_QK_ROPE, kr, 0.0).astype(jnp.bfloat16)

    for h in range(MLA_HEADS):
        base = h * MLA_QK_PAD
        q_ref[:, base:base + LANES] = (q[:, base:base + LANES] * scale).astype(jnp.bfloat16)
        qr = _rope_block(q[:, base + LANES:base + 2 * LANES], c, s1, s2, half)
        q_ref[:, base + LANES:base + 2 * LANES] = (qr * scale).astype(jnp.bfloat16)
        k_ref[:, base:base + LANES] = kn[:, h * LANES:(h + 1) * LANES].astype(jnp.bfloat16)
        k_ref[:, base + LANES:base + 2 * LANES] = kr


def _mla_up(l, z, gq, gkv, wq, wk, wv, tabs):
    m = z.shape[0]
    hq = MLA_HEADS * MLA_QK_PAD
    hv = MLA_HEADS * HEAD_DIM
    tab = pl.BlockSpec((TM_PREP, LANES), lambda i: (i, 0))
    return pl.pallas_call(
        _mla_up_kernel,
        grid=(m // TM_PREP,),
        in_specs=[pl.BlockSpec((TM_PREP, MLA_LORA), lambda i: (i, Z_CQ // MLA_LORA)),
                  pl.BlockSpec((TM_PREP, MLA_LORA), lambda i: (i, Z_CKV // MLA_LORA)),
                  pl.BlockSpec((TM_PREP, LANES), lambda i: (i, Z_KR // LANES)),
                  _layer_spec(l, (1, MLA_LORA)), _layer_spec(l, (1, MLA_LORA)),
                  _layer_spec(l, (MLA_LORA, hq)), _layer_spec(l, (MLA_LORA, hv)),
                  _layer_spec(l, (MLA_LORA, hv)),
                  tab, tab, tab],
        out_specs=[pl.BlockSpec((TM_PREP, hq), lambda i: (i, 0)),
                   pl.BlockSpec((TM_PREP, hq), lambda i: (i, 0)),
                   pl.BlockSpec((TM_PREP, hv), lambda i: (i, 0))],
        out_shape=[jax.ShapeDtypeStruct((m, hq), jnp.bfloat16),
                   jax.ShapeDtypeStruct((m, hq), jnp.bfloat16),
                   jax.ShapeDtypeStruct((m, hv), jnp.bfloat16)],
        compiler_params=_cparams(("parallel",)),
        name="mla_up",
    )(z, z, z, gq, gkv, wq, wk, wv, *tabs)


def _fox_gate_kernel(zg_ref, fb_ref, col_ref, row_ref):
    x = zg_ref[...] + fb_ref[...]
    logf = jnp.minimum(x, 0.0) - jnp.log(1.0 + jnp.exp(-jnp.abs(x)))
    s = logf.shape[0]
    row = lax.broadcasted_iota(jnp.int32, logf.shape, 0)
    d = 1
    while d < s:
        logf = logf + jnp.where(row >= d, pltpu.roll(logf, d, 0), 0.0)
        d *= 2
    logf = logf * LOG2E
    col_ref[...] = logf
    row_ref[0] = logf.T[FOX_SHIFT:FOX_SHIFT + SUBLANES, :]


def _fox_gate(l, zg, fb, batch, seq):
    return pl.pallas_call(
        _fox_gate_kernel,
        grid=(batch,),
        in_specs=[pl.BlockSpec((seq, LANES), lambda b: (b, 0)),
                  _layer_spec(l, (1, LANES))],
        out_specs=[pl.BlockSpec((seq, LANES), lambda b: (b, 0)),
                   pl.BlockSpec((1, SUBLANES, seq), lambda b: (b, 0, 0))],
        out_shape=[jax.ShapeDtypeStruct((batch * seq, LANES), jnp.float32),
                   jax.ShapeDtypeStruct((batch, SUBLANES, seq), jnp.float32)],
        compiler_params=_cparams(("parallel",)),
        name="fox_gate",
    )(zg, fb)


_NT = (((1,), (1,)), ((), ()))


def _scores(q, k_chunk, s_ref, bias_fn=None):
    n = s_ref.shape[0] // TQ
    for c in range(n):
        cs = slice(c * TQ, (c + 1) * TQ)
        s = lax.dot_general(q[c * TQ:, :], k_chunk(c), _NT, preferred_element_type=jnp.float32)
        if bias_fn is not None:
            s = s + bias_fn(c)
        s_ref[c * TQ:, cs] = s


def _softmax_tile(s_ref, p_ref, i):
    rs = slice(i * TQ, (i + 1) * TQ)

    def chunk(c):
        s = s_ref[rs, c * TQ:(c + 1) * TQ]
        if c == i:
            rows = lax.broadcasted_iota(jnp.int32, s.shape, 0)
            cols = lax.broadcasted_iota(jnp.int32, s.shape, 1)
            s = jnp.where(cols <= rows, s, -jnp.inf)
        return s

    mpart = None
    for c in range(i + 1):
        s = chunk(c)
        for b in range(TQ // LANES):
            blk = s[:, b * LANES:(b + 1) * LANES]
            mpart = blk if mpart is None else jnp.maximum(mpart, blk)
    m = jnp.max(mpart, axis=1, keepdims=True)
    for c in range(i + 1):
        p_ref[rs, c * TQ:(c + 1) * TQ] = jnp.exp2(chunk(c) - m).astype(jnp.bfloat16)


def _fill_values(va_ref, v):
    va_ref[:, :HEAD_DIM] = v
    va_ref[:, HEAD_DIM:] = jnp.ones((va_ref.shape[0], va_ref.shape[1] - HEAD_DIM), va_ref.dtype)


def _attend(p_ref, va_ref, i):
    n = (i + 1) * TQ
    o = jnp.dot(p_ref[i * TQ:(i + 1) * TQ, :n], va_ref[:n, :], preferred_element_type=jnp.float32)
    return o[:, :HEAD_DIM] / o[:, HEAD_DIM:]


def _rows(ref, c, cols):
    return ref[c * TQ:(c + 1) * TQ, cols]


def _mla_attn_kernel(q_ref, k_ref, v_ref, o_ref, s_ref, p_ref, va_ref):
    for g in range(HEADS_PER_STEP):
        qs = slice(g * MLA_QK_PAD, (g + 1) * MLA_QK_PAD)
        vs = slice(g * HEAD_DIM, (g + 1) * HEAD_DIM)
        _fill_values(va_ref, v_ref[:, vs])
        _scores(q_ref[:, qs], lambda c: _rows(k_ref, c, qs), s_ref)
        for i in range(q_ref.shape[0] // TQ):
            _softmax_tile(s_ref, p_ref, i)
            o_ref[i * TQ:(i + 1) * TQ, vs] = _attend(p_ref, va_ref, i).astype(o_ref.dtype)


def _unshift(lo, hi):
    return jnp.concatenate([lo[:, FOX_SHIFT:], hi[:, :FOX_SHIFT]], axis=1)


def _fox_attn_kernel(*refs):
    nb = HEADS_PER_STEP + 1
    qb, kb, vb = refs[:nb], refs[nb:2 * nb], refs[2 * nb:3 * nb]
    ccol_ref, crow_ref, o_ref, s_ref, p_ref, va_ref, k_ref = refs[3 * nb:]
    scale = HEAD_DIM ** -0.5 * LOG2E
    cc = ccol_ref[...]
    lane = lax.broadcasted_iota(jnp.int32, cc.shape, 1)
    all_cols = slice(None)
    for g in range(HEADS_PER_STEP):
        h = pl.program_id(1) * HEADS_PER_STEP + g
        vs = slice(g * HEAD_DIM, (g + 1) * HEAD_DIM)
        k_ref[...] = _unshift(kb[g][...], kb[g + 1][...])
        _fill_values(va_ref, _unshift(vb[g][...], vb[g + 1][...]))
        q = _unshift(qb[g][...], qb[g + 1][...])
        q = (q.astype(jnp.float32) * scale).astype(jnp.bfloat16)
        crow = crow_ref[0, pl.ds(h, 1), :]
        ccq = jnp.sum(jnp.where(lane == h + FOX_SHIFT, cc, 0.0), axis=1, keepdims=True)
        _scores(q, lambda c: _rows(k_ref, c, all_cols), s_ref,
                bias_fn=lambda c: ccq[c * TQ:, :] - crow[:, c * TQ:(c + 1) * TQ])
        for i in range(o_ref.shape[0] // TQ):
            _softmax_tile(s_ref, p_ref, i)
            o_ref[i * TQ:(i + 1) * TQ, vs] = _attend(p_ref, va_ref, i).astype(o_ref.dtype)


def _diff_attn_kernel(lambda_init, q_ref, k_ref, v_ref, lam_ref, g_ref, o_ref, s_ref, p_ref, va_ref, o1_ref):
    lv = lam_ref[...]
    lam = (jnp.exp(jnp.sum(lv[0:1] * lv[1:2], axis=1, keepdims=True))
           - jnp.exp(jnp.sum(lv[2:3] * lv[3:4], axis=1, keepdims=True)) + lambda_init)
    gain = g_ref[...] * (1.0 - lambda_init)
    nq = q_ref.shape[0] // TQ
    for g in range(HEADS_PER_STEP):
        vs = slice(g * HEAD_DIM, (g + 1) * HEAD_DIM)
        q = q_ref[:, vs]
        lane = lax.broadcasted_iota(jnp.int32, q.shape, 1)
        zero = jnp.zeros_like(q)
        k_chunk = lambda c: _rows(k_ref, c, vs)
        _fill_values(va_ref, v_ref[:, vs])
        _scores(jnp.where(lane < DIFF_QK_DIM, q, zero), k_chunk, s_ref)
        for i in range(nq):
            _softmax_tile(s_ref, p_ref, i)
            o1_ref[i * TQ:(i + 1) * TQ, :] = _attend(p_ref, va_ref, i)
        _scores(jnp.where(lane >= DIFF_QK_DIM, q, zero), k_chunk, s_ref)
        for i in range(nq):
            rs = slice(i * TQ, (i + 1) * TQ)
            _softmax_tile(s_ref, p_ref, i)
            o = o1_ref[rs, :] - lam * _attend(p_ref, va_ref, i)
            o_ref[rs, vs] = (o * _rms_scale(o, SUBLN_EPS) * gain).astype(o_ref.dtype)


def _attn_call(kernel, name, batch, seq, heads, ins, in_specs, extra_scratch=()):
    return pl.pallas_call(
        kernel,
        grid=(batch, heads // HEADS_PER_STEP),
        in_specs=in_specs,
        out_specs=pl.BlockSpec((seq, HEADS_PER_STEP * HEAD_DIM), lambda b, h: (b, h)),
        out_shape=jax.ShapeDtypeStruct((batch * seq, heads * HEAD_DIM), jnp.bfloat16),
        scratch_shapes=[pltpu.VMEM((seq, seq), jnp.float32), pltpu.VMEM((seq, seq), jnp.bfloat16),
                        pltpu.VMEM((seq, 2 * HEAD_DIM), jnp.bfloat16), *extra_scratch],
        compiler_params=_cparams(("parallel", "parallel")),
        name=name,
    )(*ins)


def _colblk(seq, width, col0):
    gw = HEADS_PER_STEP * width
    assert col0 % gw == 0
    return pl.BlockSpec((seq, gw), lambda b, h: (b, col0 // gw + h))


def _diff_attn(l, z, lam, gain, lambda_init, batch, seq):
    kern = functools.partial(_diff_attn_kernel, lambda_init)
    specs = [_colblk(seq, HEAD_DIM, Z_AQ), _colblk(seq, HEAD_DIM, Z_AK), _colblk(seq, HEAD_DIM, Z_AV),
             _layer_spec(l, (4, DIFF_QK_DIM)), _layer_spec(l, (1, HEAD_DIM))]
    return _attn_call(kern, "diff_attn", batch, seq, DIFF_HEADS, (z, z, z, lam, gain), specs,
                      extra_scratch=(pltpu.VMEM((seq, HEAD_DIM), jnp.float32),))


def _mla_attn(q, k, v, batch, seq):
    specs = [_colblk(seq, MLA_QK_PAD, 0), _colblk(seq, MLA_QK_PAD, 0), _colblk(seq, HEAD_DIM, 0)]
    return _attn_call(_mla_attn_kernel, "mla_attn", batch, seq, MLA_HEADS, (q, k, v), specs)


def _fox_attn(z, ccol, crow, batch, seq):
    nb = HEADS_PER_STEP + 1
    specs = []
    for col0 in (Z_FQ, Z_FK, Z_FV):
        blk0 = (col0 - FOX_SHIFT) // LANES
        specs += [pl.BlockSpec((seq, LANES), lambda b, h, o=blk0 + t: (b, o + HEADS_PER_STEP * h))
                  for t in range(nb)]
    specs += [pl.BlockSpec((seq, LANES), lambda b, h: (b, 0)),
              pl.BlockSpec((1, SUBLANES, seq), lambda b, h: (b, 0, 0))]
    kv = pltpu.VMEM((seq, HEAD_DIM), jnp.bfloat16)
    return _attn_call(_fox_attn_kernel, "fox_attn", batch, seq, FOX_HEADS, (z,) * (3 * nb) + (ccol, crow),
                      specs, extra_scratch=(kv,))


def _out_proj_kernel(x_ref, a_ref, b_ref, c_ref, wb_ref, o_ref):
    na, nb = a_ref.shape[1], b_ref.shape[1]
    acc = jnp.dot(a_ref[...], wb_ref[:na, :], preferred_element_type=jnp.float32)
    acc = acc + jnp.dot(b_ref[...], wb_ref[na:na + nb, :], preferred_element_type=jnp.float32)
    acc = acc + jnp.dot(c_ref[...], wb_ref[na + nb:, :], preferred_element_type=jnp.float32)
    o_ref[...] = x_ref[...] + acc


def _out_proj(l, x, oa, ob, oc, w_o):
    m, d = x.shape
    dm = w_o.shape[1]
    assert oa.shape[1] + ob.shape[1] + oc.shape[1] == dm
    lhs = lambda a: pl.BlockSpec((TM_OUT, a.shape[1]), lambda i: (i, 0))
    xo = pl.BlockSpec((TM_OUT, d), lambda i: (i, 0))
    return pl.pallas_call(
        _out_proj_kernel,
        grid=(m // TM_OUT,),
        in_specs=[xo, lhs(oa), lhs(ob), lhs(oc), _layer_spec(l, (dm, d))],
        out_specs=xo,
        out_shape=jax.ShapeDtypeStruct((m, d), jnp.float32),
        compiler_params=_cparams(("parallel",)),
        name="out_proj",
    )(x, oa, ob, oc, w_o)


def _ffn_up_kernel(tiles_per_seq, x_ref, g_ref, wa_ref, wg_ref, cwa_ref, cwg_ref, cba_ref, cbg_ref,
                   o_ref, h_ref, ua_ref, ug_ref, carry_ref):
    i, j = pl.program_id(0), pl.program_id(1)
    tm = x_ref.shape[0]

    @pl.when(j == 0)
    def _():
        x = x_ref[...]
        h_ref[...] = (x * _rms_scale(x, NORM_EPS) * g_ref[...]).astype(jnp.bfloat16)

    @pl.when(i % tiles_per_seq == 0)
    def _():
        carry_ref[j] = jnp.zeros(carry_ref.shape[1:], jnp.float32)

    ua_ref[0:SUBLANES, :] = carry_ref[j, 0]
    ug_ref[0:SUBLANES, :] = carry_ref[j, 1]

    def conv(u_ref, w_ref, half, cw_ref, cb_ref):
        u = jnp.dot(h_ref[...], w_ref[...].astype(jnp.bfloat16), preferred_element_type=jnp.float32)
        u_ref[SUBLANES:, :] = u
        carry_ref[j, half] = u[tm - SUBLANES:, :]
        cw = cw_ref[...]
        return (u * cw[2:3] + u_ref[SUBLANES - 1:SUBLANES - 1 + tm, :] * cw[1:2]
                + u_ref[SUBLANES - 2:SUBLANES - 2 + tm, :] * cw[0:1] + cb_ref[...])

    g = conv(ug_ref, wg_ref, 1, cwg_ref, cbg_ref)
    gate = g / (1.0 + jnp.exp(-g))
    a = conv(ua_ref, wa_ref, 0, cwa_ref, cba_ref)
    o_ref[...] = (gate * a).astype(o_ref.dtype)


def _ffn_up(l, x, gain, w_up, conv_w, conv_b, seq):
    m, d = x.shape
    nj = D_FF // TN_FF
    kern = functools.partial(_ffn_up_kernel, seq // TM_PROJ)
    col = lambda rows, off: _layer_spec(l, (rows, TN_FF), lambda i, j: j + off)
    return pl.pallas_call(
        kern,
        grid=(m // TM_PROJ, nj),
        in_specs=[pl.BlockSpec((TM_PROJ, d), lambda i, j: (i, 0)),
                  _layer_spec(l, (1, d)),
                  col(d, 0), col(d, nj), col(3, 0), col(3, nj), col(1, 0), col(1, nj)],
        out_specs=pl.BlockSpec((TM_PROJ, TN_FF), lambda i, j: (i, j)),
        out_shape=jax.ShapeDtypeStruct((m, D_FF), jnp.bfloat16),
        scratch_shapes=[pltpu.VMEM((TM_PROJ, d), jnp.bfloat16),
                        pltpu.VMEM((TM_PROJ + SUBLANES, TN_FF), jnp.float32),
                        pltpu.VMEM((TM_PROJ + SUBLANES, TN_FF), jnp.float32),
                        pltpu.VMEM((nj, 2, SUBLANES, TN_FF), jnp.float32)],
        compiler_params=_cparams(("arbitrary", "arbitrary")),
        name="ffn_up",
    )(x, gain, w_up, w_up, conv_w, conv_w, conv_b, conv_b)


def _ffn_down_kernel(x_ref, a_ref, w_ref, o_ref):
    o_ref[...] = x_ref[...] + jnp.dot(a_ref[...], w_ref[...], preferred_element_type=jnp.float32)


def _ffn_down(l, x, act, w):
    m, d = x.shape
    k = act.shape[1]
    xo = pl.BlockSpec((TM_PROJ, TN_DOWN), lambda i, j: (i, j))
    return pl.pallas_call(
        _ffn_down_kernel,
        grid=(m // TM_PROJ, d // TN_DOWN),
        in_specs=[xo, pl.BlockSpec((TM_PROJ, k), lambda i, j: (i, 0)),
                  _layer_spec(l, (k, TN_DOWN), lambda i, j: j)],
        out_specs=xo,
        out_shape=jax.ShapeDtypeStruct((m, d), jnp.float32),
        compiler_params=_cparams(("parallel", "parallel")),
        name="ffn_down",
    )(x, act, w)


def _final_norm_kernel(x_ref, g_ref, o_ref):
    x = x_ref[...]
    o_ref[...] = x * _rms_scale(x, NORM_EPS) * g_ref[...]


def _final_norm(x, gain):
    m, d = x.shape
    blk = pl.BlockSpec((TM_PREP, d), lambda i: (i, 0))
    return pl.pallas_call(
        _final_norm_kernel,
        grid=(m // TM_PREP,),
        in_specs=[blk, pl.BlockSpec((1, d), lambda i: (0, 0))],
        out_specs=blk,
        out_shape=jax.ShapeDtypeStruct((m, d), jnp.float32),
        compiler_params=_cparams(("parallel",)),
        name="final_norm",
    )(x, gain)


def _w_in_prep_kernel(w_ref, t_ref, o_ref):
    c = pl.program_id(0)
    n_main = pl.num_programs(0) - t_ref.shape[2] // LANES

    @pl.when(c < n_main)
    def _():
        for l in range(o_ref.shape[0]):
            o_ref[l] = w_ref[:, l, :].T.astype(jnp.bfloat16)

    for t in range(t_ref.shape[2] // LANES):
        @pl.when(c == n_main + t)
        def _():
            o_ref[...] = t_ref[:, :, t * LANES:(t + 1) * LANES]


def _w_in_prep(w, width):
    dp, k, n = w.shape
    n_main = n // LANES
    tail_w = width - n_main * LANES
    assert width % LANES == 0 and tail_w >= n - n_main * LANES
    wt = jnp.transpose(w, (2, 0, 1))
    tail = jnp.pad(w[:, :, n_main * LANES:], ((0, 0), (0, 0), (0, tail_w - (n - n_main * LANES))))
    return pl.pallas_call(
        _w_in_prep_kernel,
        grid=(width // LANES,),
        in_specs=[pl.BlockSpec((LANES, dp, k), lambda c: (jnp.minimum(c, n_main - 1), 0, 0)),
                  pl.BlockSpec((dp, k, tail_w), lambda c: (0, 0, 0))],
        out_specs=pl.BlockSpec((dp, k, LANES), lambda c: (0, 0, c)),
        out_shape=jax.ShapeDtypeStruct((dp, k, width), jnp.bfloat16),
        compiler_params=_cparams(("arbitrary",)),
        name="w_in_prep",
    )(wt, tail.astype(jnp.bfloat16))


def _pad_w_uq(w):
    dp, r, _ = w.shape
    w = w.reshape(dp, r, MLA_HEADS, MLA_QK_NOPE + MLA_QK_ROPE)
    w = jnp.pad(w, ((0, 0), (0, 0), (0, 0), (0, MLA_QK_PAD - MLA_QK_NOPE - MLA_QK_ROPE)))
    return w.reshape(dp, r, MLA_HEADS * MLA_QK_PAD).astype(jnp.bfloat16)


def _split_w_ukv(w):
    dp, r, _ = w.shape
    w = w.reshape(dp, r, MLA_HEADS, MLA_QK_NOPE + HEAD_DIM)
    wk = w[..., :MLA_QK_NOPE].reshape(dp, r, MLA_HEADS * MLA_QK_NOPE)
    wv = w[..., MLA_QK_NOPE:].reshape(dp, r, MLA_HEADS * HEAD_DIM)
    return wk.astype(jnp.bfloat16), wv.astype(jnp.bfloat16)


def kernel(x, positions, attn_norm, w_in, diff_lambda, diff_out_norm, mla_q_norm, mla_kv_norm, mla_w_uq,
           mla_w_ukv, fox_forget_bias, w_o, ffn_norm, ffn_w_up, ffn_conv_w, ffn_conv_b, ffn_w_down,
           final_norm):
    batch, seq, d = x.shape
    depth = w_in.shape[0]
    m = batch * seq
    assert seq % TQ == 0 and seq % TM_PROJ == 0 and m % TM_PROJ == 0
    assert w_in.shape[2] == IN_WIDTH

    pat_a, pat_b = _rope_lane_patterns()
    pos_col = positions.reshape(m, 1)
    tabs_a = _rope_tables(pos_col, *pat_a)
    tabs_b = _rope_tables(pos_col, *pat_b)

    bf = lambda a: a.astype(jnp.bfloat16)
    rows = lambda a: a.reshape(depth, 1, -1)
    w_in_b = _w_in_prep(w_in, Z_WIDTH)
    w_uq_b = _pad_w_uq(mla_w_uq)
    w_uk_b, w_uv_b = _split_w_ukv(mla_w_ukv)
    w_o_b, w_down_b = bf(w_o), bf(ffn_w_down)
    fb = jnp.pad(fox_forget_bias, ((0, 0), (FOX_SHIFT, LANES - FOX_SHIFT - FOX_HEADS))).reshape(depth, 1, LANES)
    g_attn, g_ffn, g_q, g_kv, g_diff = (rows(attn_norm), rows(ffn_norm), rows(mla_q_norm),
                                        rows(mla_kv_norm), rows(diff_out_norm))
    conv_b = rows(ffn_conv_b)

    xf = x.reshape(m, d)
    for l in range(depth):
        lambda_init = 0.8 - 0.6 * math.exp(-0.3 * l)
        z, zg = _in_proj(l, xf, g_attn, w_in_b, tabs_a)
        q_b, k_b, v_b = _mla_up(l, z, g_q, g_kv, w_uq_b, w_uk_b, w_uv_b, tabs_b)
        ccol, crow = _fox_gate(l, zg, fb, batch, seq)
        o_a = _diff_attn(l, z, diff_lambda, g_diff, lambda_init, batch, seq)
        o_b = _mla_attn(q_b, k_b, v_b, batch, seq)
        o_c = _fox_attn(z, ccol, crow, batch, seq)
        xf = _out_proj(l, xf, o_a, o_b, o_c, w_o_b)
        act = _ffn_up(l, xf, g_ffn, ffn_w_up, ffn_conv_w, conv_b, seq)
        xf = _ffn_down(l, xf, act, w_down_b)
    return _final_norm(xf, final_norm.reshape(1, d)).reshape(batch, seq, d)
```

```python
import functools
import math

import numpy as np
import jax
import jax.numpy as jnp
from jax import lax
from jax.experimental import pallas as pl
from jax.experimental.pallas import tpu as pltpu

HEAD_DIM = 128
DIFF_HEADS = 4
DIFF_QK_DIM = 64
MLA_HEADS = 6
MLA_LORA = 512
MLA_QK_NOPE = 128
MLA_QK_ROPE = 64
FOX_HEADS = 6
D_FF = 5632
ROPE_THETA = 500000.0
PARTIAL_ROT_DIM = DIFF_QK_DIM // 4
NORM_EPS = 1e-6
SUBLN_EPS = 1e-5
LOG2E = math.log2(math.e)

LANES = 128
SUBLANES = 8
VMEM_LIMIT_BYTES = 56 * 1024 * 1024

Z_AQ, Z_AK, Z_AV = 0, 512, 1024
Z_CQ, Z_CKV, Z_KR = 1536, 2048, 2560
Z_FQ, Z_FK, Z_FV = 2624, 3392, 4160
Z_FG = 4928
IN_WIDTH = Z_FG + FOX_HEADS
FOX_SHIFT = Z_FQ % LANES
assert Z_FK % LANES == FOX_SHIFT and Z_FV % LANES == FOX_SHIFT and Z_FG % LANES == FOX_SHIFT
MLA_QK_PAD = 256

TM_PROJ = 1024
TN_IN = 1280
TM_OUT = 512
TN_FF = 512
TN_DOWN = 512
TM_PREP = 1024
TQ = 256
Z_WIDTH = -(-IN_WIDTH // TN_IN) * TN_IN


def _cparams(sem):
    return pltpu.CompilerParams(dimension_semantics=sem, vmem_limit_bytes=VMEM_LIMIT_BYTES)


def _rms_scale(x32, eps):
    return lax.rsqrt(jnp.mean(x32 * x32, axis=-1, keepdims=True) + eps)


def _layer_spec(l, shape, col=None):
    if col is None:
        return pl.BlockSpec((None,) + shape, lambda *g: (l, 0, 0))
    return pl.BlockSpec((None,) + shape, lambda *g: (l, 0, col(*g)))


def _rope_table_kernel(pos_ref, invf_ref, m1_ref, m2_ref, c_ref, s1_ref, s2_ref):
    ang = pos_ref[...].astype(jnp.float32) * invf_ref[...]
    sn = jnp.sin(ang)
    c_ref[...] = jnp.cos(ang)
    s1_ref[...] = sn * m1_ref[...]
    s2_ref[...] = sn * m2_ref[...]


def _rope_tables(pos_col, invf, m1, m2):
    m = pos_col.shape[0]
    row = pl.BlockSpec((1, LANES), lambda i: (0, 0))
    tab = pl.BlockSpec((TM_PREP, LANES), lambda i: (i, 0))
    return pl.pallas_call(
        _rope_table_kernel,
        grid=(m // TM_PREP,),
        in_specs=[pl.BlockSpec((TM_PREP, 1), lambda i: (i, 0)), row, row, row],
        out_specs=[tab, tab, tab],
        out_shape=[jax.ShapeDtypeStruct((m, LANES), jnp.float32)] * 3,
        compiler_params=_cparams(("parallel",)),
        name="rope_tables",
    )(pos_col, invf, m1, m2)


def _rope_lane_patterns():
    lane = np.arange(LANES)
    half_a = PARTIAL_ROT_DIM // 2
    r = lane % DIFF_QK_DIM
    inv_a = ROPE_THETA ** (-jnp.arange(0, PARTIAL_ROT_DIM, 2, dtype=jnp.float32) / PARTIAL_ROT_DIM)
    invf_a = jnp.where(jnp.asarray(r < PARTIAL_ROT_DIM), inv_a[jnp.asarray(r % half_a)], 0.0)
    m1_a = np.where(r < half_a, -1.0, 0.0)
    m2_a = np.where((r >= half_a) & (r < PARTIAL_ROT_DIM), 1.0, 0.0)
    half_b = MLA_QK_ROPE // 2
    inv_b = ROPE_THETA ** (-jnp.arange(0, MLA_QK_ROPE, 2, dtype=jnp.float32) / MLA_QK_ROPE)
    invf_b = jnp.where(jnp.asarray(lane < MLA_QK_ROPE), inv_b[jnp.asarray(lane % half_b)], 0.0)
    m1_b = np.where(lane < half_b, -1.0, 0.0)
    m2_b = np.where((lane >= half_b) & (lane < MLA_QK_ROPE), 1.0, 0.0)
    f = lambda a: jnp.asarray(a, jnp.float32).reshape(1, LANES)
    return (f(invf_a), f(m1_a), f(m2_a)), (f(invf_b), f(m1_b), f(m2_b))


def _rope_block(x, c, s1, s2, d):
    return x * c + pltpu.roll(x, LANES - d, 1) * s1 + pltpu.roll(x, d, 1) * s2


def _in_proj_kernel(x_ref, g_ref, w_ref, c_ref, s1_ref, s2_ref, z_ref, zg_ref, h_ref):
    j = pl.program_id(1)

    @pl.when(j == 0)
    def _():
        x = x_ref[...]
        h_ref[...] = (x * _rms_scale(x, NORM_EPS) * g_ref[...]).astype(jnp.bfloat16)

    @pl.when(j == 0)
    def _():
        c, s1, s2 = c_ref[...], s1_ref[...], s2_ref[...]
        scale = DIFF_QK_DIM ** -0.5 * LOG2E
        acc = jnp.dot(h_ref[...], w_ref[...], preferred_element_type=jnp.float32)
        for b in range(Z_AV // LANES):
            sl = slice(b * LANES, (b + 1) * LANES)
            r = _rope_block(acc[:, sl], c, s1, s2, PARTIAL_ROT_DIM // 2)
            z_ref[:, sl] = ((r * scale) if b * LANES < Z_AK else r).astype(jnp.bfloat16)
        z_ref[:, Z_AV:] = acc[:, Z_AV:].astype(jnp.bfloat16)

    @pl.when(j != 0)
    def _():
        acc = jnp.dot(h_ref[...], w_ref[...], preferred_element_type=jnp.float32)
        z_ref[...] = acc.astype(jnp.bfloat16)

        @pl.when(j == Z_FG // TN_IN)
        def _():
            off = Z_FG % TN_IN // LANES * LANES
            zg_ref[...] = acc[:, off:off + LANES]


def _in_proj(l, x, gain, w, tabs):
    m, d = x.shape
    assert Z_AQ == 0 and TN_IN >= Z_AV
    tab = pl.BlockSpec((TM_PROJ, LANES), lambda i, j: (i, 0))
    return pl.pallas_call(
        _in_proj_kernel,
        grid=(m // TM_PROJ, Z_WIDTH // TN_IN),
        in_specs=[pl.BlockSpec((TM_PROJ, d), lambda i, j: (i, 0)),
                  _layer_spec(l, (1, d)),
                  _layer_spec(l, (d, TN_IN), lambda i, j: j),
                  tab, tab, tab],
        out_specs=[pl.BlockSpec((TM_PROJ, TN_IN), lambda i, j: (i, j)),
                   pl.BlockSpec((TM_PROJ, LANES), lambda i, j: (i, 0))],
        out_shape=[jax.ShapeDtypeStruct((m, Z_WIDTH), jnp.bfloat16),
                   jax.ShapeDtypeStruct((m, LANES), jnp.float32)],
        scratch_shapes=[pltpu.VMEM((TM_PROJ, d), jnp.bfloat16)],
        compiler_params=_cparams(("parallel", "arbitrary")),
        name="in_proj",
    )(x, gain, w, *tabs)


def _mla_up_kernel(cq_ref, ckv_ref, kr_ref, gq_ref, gkv_ref, wq_ref, wk_ref, wv_ref,
                   c_ref, s1_ref, s2_ref, q_ref, k_ref, v_ref):
    c, s1, s2 = c_ref[...], s1_ref[...], s2_ref[...]
    half = MLA_QK_ROPE // 2
    scale = (MLA_QK_NOPE + MLA_QK_ROPE) ** -0.5 * LOG2E

    cq = cq_ref[...].astype(jnp.float32)
    cqn = (cq * _rms_scale(cq, NORM_EPS) * gq_ref[...]).astype(jnp.bfloat16)
    q = jnp.dot(cqn, wq_ref[...], preferred_element_type=jnp.float32)
    ckv = ckv_ref[...].astype(jnp.float32)
    ckvn = (ckv * _rms_scale(ckv, NORM_EPS) * gkv_ref[...]).astype(jnp.bfloat16)
    kn = jnp.dot(ckvn, wk_ref[...], preferred_element_type=jnp.float32)
    v_ref[...] = jnp.dot(ckvn, wv_ref[...], preferred_element_type=jnp.float32).astype(jnp.bfloat16)
    kr = _rope_block(kr_ref[...].astype(jnp.float32), c, s1, s2, half)
    lane = lax.broadcasted_iota(jnp.int32, kr.shape, 1)
    kr = jnp.where(lane < MLA_QK_ROPE, kr, 0.0).astype(jnp.bfloat16)

    for h in range(MLA_HEADS):
        base = h * MLA_QK_PAD
        q_ref[:, base:base + LANES] = (q[:, base:base + LANES] * scale).astype(jnp.bfloat16)
        qr = _rope_block(q[:, base + LANES:base + 2 * LANES], c, s1, s2, half)
        q_ref[:, base + LANES:base + 2 * LANES] = (qr * scale).astype(jnp.bfloat16)
        k_ref[:, base:base + LANES] = kn[:, h * LANES:(h + 1) * LANES].astype(jnp.bfloat16)
        k_ref[:, base + LANES:base + 2 * LANES] = kr


def _mla_up(l, z, gq, gkv, wq, wk, wv, tabs):
    m = z.shape[0]
    hq = MLA_HEADS * MLA_QK_PAD
    hv = MLA_HEADS * HEAD_DIM
    tab = pl.BlockSpec((TM_PREP, LANES), lambda i: (i, 0))
    return pl.pallas_call(
        _mla_up_kernel,
        grid=(m // TM_PREP,),
        in_specs=[pl.BlockSpec((TM_PREP, MLA_LORA), lambda i: (i, Z_CQ // MLA_LORA)),
                  pl.BlockSpec((TM_PREP, MLA_LORA), lambda i: (i, Z_CKV // MLA_LORA)),
                  pl.BlockSpec((TM_PREP, LANES), lambda i: (i, Z_KR // LANES)),
                  _layer_spec(l, (1, MLA_LORA)), _layer_spec(l, (1, MLA_LORA)),
                  _layer_spec(l, (MLA_LORA, hq)), _layer_spec(l, (MLA_LORA, hv)),
                  _layer_spec(l, (MLA_LORA, hv)),
                  tab, tab, tab],
        out_specs=[pl.BlockSpec((TM_PREP, hq), lambda i: (i, 0)),
                   pl.BlockSpec((TM_PREP, hq), lambda i: (i, 0)),
                   pl.BlockSpec((TM_PREP, hv), lambda i: (i, 0))],
        out_shape=[jax.ShapeDtypeStruct((m, hq), jnp.bfloat16),
                   jax.ShapeDtypeStruct((m, hq), jnp.bfloat16),
                   jax.ShapeDtypeStruct((m, hv), jnp.bfloat16)],
        compiler_params=_cparams(("parallel",)),
        name="mla_up",
    )(z, z, z, gq, gkv, wq, wk, wv, *tabs)


def _fox_gate_kernel(zg_ref, fb_ref, col_ref, row_ref):
    x = zg_ref[...] + fb_ref[...]
    logf = jnp.minimum(x, 0.0) - jnp.log(1.0 + jnp.exp(-jnp.abs(x)))
    s = logf.shape[0]
    row = lax.broadcasted_iota(jnp.int32, logf.shape, 0)
    d = 1
    while d < s:
        logf = logf + jnp.where(row >= d, pltpu.roll(logf, d, 0), 0.0)
        d *= 2
    logf = logf * LOG2E
    col_ref[...] = logf
    row_ref[0] = logf.T[FOX_SHIFT:FOX_SHIFT + SUBLANES, :]


def _fox_gate(l, zg, fb, batch, seq):
    return pl.pallas_call(
        _fox_gate_kernel,
        grid=(batch,),
        in_specs=[pl.BlockSpec((seq, LANES), lambda b: (b, 0)),
                  _layer_spec(l, (1, LANES))],
        out_specs=[pl.BlockSpec((seq, LANES), lambda b: (b, 0)),
                   pl.BlockSpec((1, SUBLANES, seq), lambda b: (b, 0, 0))],
        out_shape=[jax.ShapeDtypeStruct((batch * seq, LANES), jnp.float32),
                   jax.ShapeDtypeStruct((batch, SUBLANES, seq), jnp.float32)],
        compiler_params=_cparams(("parallel",)),
        name="fox_gate",
    )(zg, fb)


_NT = (((1,), (1,)), ((), ()))


def _scores(q, k_ref, s_ref, bias_fn=None):
    n = s_ref.shape[0] // TQ
    for c in range(n):
        cs = slice(c * TQ, (c + 1) * TQ)
        s = lax.dot_general(q[c * TQ:, :], k_ref[cs, :], _NT, preferred_element_type=jnp.float32)
        if bias_fn is not None:
            s = s + bias_fn(c)
        s_ref[c * TQ:, cs] = s


def _softmax_tile(s_ref, p_ref, i):
    rs = slice(i * TQ, (i + 1) * TQ)

    def chunk(c):
        s = s_ref[rs, c * TQ:(c + 1) * TQ]
        if c == i:
            rows = lax.broadcasted_iota(jnp.int32, s.shape, 0)
            cols = lax.broadcasted_iota(jnp.int32, s.shape, 1)
            s = jnp.where(cols <= rows, s, -jnp.inf)
        return s

    mpart = None
    for c in range(i + 1):
        s = chunk(c)
        for b in range(TQ // LANES):
            blk = s[:, b * LANES:(b + 1) * LANES]
            mpart = blk if mpart is None else jnp.maximum(mpart, blk)
    m = jnp.max(mpart, axis=1, keepdims=True)
    for c in range(i + 1):
        p_ref[rs, c * TQ:(c + 1) * TQ] = jnp.exp2(chunk(c) - m).astype(jnp.bfloat16)


def _fill_values(va_ref, v):
    va_ref[:, :HEAD_DIM] = v
    va_ref[:, HEAD_DIM:] = jnp.ones((va_ref.shape[0], va_ref.shape[1] - HEAD_DIM), va_ref.dtype)


def _attend(p_ref, va_ref, i):
    n = (i + 1) * TQ
    o = jnp.dot(p_ref[i * TQ:(i + 1) * TQ, :n], va_ref[:n, :], preferred_element_type=jnp.float32)
    return o[:, :HEAD_DIM] / o[:, HEAD_DIM:]


def _mla_attn_kernel(q_ref, k_ref, v_ref, o_ref, s_ref, p_ref, va_ref):
    _fill_values(va_ref, v_ref[...])
    _scores(q_ref[...], k_ref, s_ref)
    for i in range(q_ref.shape[0] // TQ):
        _softmax_tile(s_ref, p_ref, i)
        o_ref[i * TQ:(i + 1) * TQ, :] = _attend(p_ref, va_ref, i).astype(o_ref.dtype)


def _unshift(lo, hi):
    return jnp.concatenate([lo[:, FOX_SHIFT:], hi[:, :FOX_SHIFT]], axis=1)


def _fox_attn_kernel(qlo_ref, qhi_ref, klo_ref, khi_ref, vlo_ref, vhi_ref, ccol_ref, crow_ref,
                     o_ref, s_ref, p_ref, va_ref, k_ref):
    h = pl.program_id(1)
    scale = HEAD_DIM ** -0.5 * LOG2E
    k_ref[...] = _unshift(klo_ref[...], khi_ref[...])
    _fill_values(va_ref, _unshift(vlo_ref[...], vhi_ref[...]))
    q = _unshift(qlo_ref[...], qhi_ref[...])
    q = (q.astype(jnp.float32) * scale).astype(jnp.bfloat16)
    crow = crow_ref[0, pl.ds(h, 1), :]
    cc = ccol_ref[...]
    lane = lax.broadcasted_iota(jnp.int32, cc.shape, 1)
    ccq = jnp.sum(jnp.where(lane == h + FOX_SHIFT, cc, 0.0), axis=1, keepdims=True)
    _scores(q, k_ref, s_ref, bias_fn=lambda c: ccq[c * TQ:, :] - crow[:, c * TQ:(c + 1) * TQ])
    for i in range(o_ref.shape[0] // TQ):
        _softmax_tile(s_ref, p_ref, i)
        o_ref[i * TQ:(i + 1) * TQ, :] = _attend(p_ref, va_ref, i).astype(o_ref.dtype)


def _diff_attn_kernel(lambda_init, q_ref, k_ref, v_ref, lam_ref, g_ref, o_ref, s_ref, p_ref, va_ref, o1_ref):
    lv = lam_ref[...]
    lam = (jnp.exp(jnp.sum(lv[0:1] * lv[1:2], axis=1, keepdims=True))
           - jnp.exp(jnp.sum(lv[2:3] * lv[3:4], axis=1, keepdims=True)) + lambda_init)
    gain = g_ref[...] * (1.0 - lambda_init)
    q = q_ref[...]
    lane = lax.broadcasted_iota(jnp.int32, q.shape, 1)
    zero = jnp.zeros_like(q)
    nq = q_ref.shape[0] // TQ
    _fill_values(va_ref, v_ref[...])
    _scores(jnp.where(lane < DIFF_QK_DIM, q, zero), k_ref, s_ref)
    for i in range(nq):
        _softmax_tile(s_ref, p_ref, i)
        o1_ref[i * TQ:(i + 1) * TQ, :] = _attend(p_ref, va_ref, i)
    _scores(jnp.where(lane >= DIFF_QK_DIM, q, zero), k_ref, s_ref)
    for i in range(nq):
        rs = slice(i * TQ, (i + 1) * TQ)
        _softmax_tile(s_ref, p_ref, i)
        o = o1_ref[rs, :] - lam * _attend(p_ref, va_ref, i)
        o_ref[rs, :] = (o * _rms_scale(o, SUBLN_EPS) * gain).astype(o_ref.dtype)


def _attn_call(kernel, name, batch, seq, heads, ins, in_specs, extra_scratch=()):
    return pl.pallas_call(
        kernel,
        grid=(batch, heads),
        in_specs=in_specs,
        out_specs=pl.BlockSpec((seq, HEAD_DIM), lambda b, h: (b, h)),
        out_shape=jax.ShapeDtypeStruct((batch * seq, heads * HEAD_DIM), jnp.bfloat16),
        scratch_shapes=[pltpu.VMEM((seq, seq), jnp.float32), pltpu.VMEM((seq, seq), jnp.bfloat16),
                        pltpu.VMEM((seq, 2 * HEAD_DIM), jnp.bfloat16), *extra_scratch],
        compiler_params=_cparams(("parallel", "parallel")),
        name=name,
    )(*ins)


def _colblk(seq, width, col0):
    assert col0 % width == 0
    return pl.BlockSpec((seq, width), lambda b, h: (b, col0 // width + h))


def _diff_attn(l, z, lam, gain, lambda_init, batch, seq):
    kern = functools.partial(_diff_attn_kernel, lambda_init)
    specs = [_colblk(seq, HEAD_DIM, Z_AQ), _colblk(seq, HEAD_DIM, Z_AK), _colblk(seq, HEAD_DIM, Z_AV),
             _layer_spec(l, (4, DIFF_QK_DIM)), _layer_spec(l, (1, HEAD_DIM))]
    return _attn_call(kern, "diff_attn", batch, seq, DIFF_HEADS, (z, z, z, lam, gain), specs,
                      extra_scratch=(pltpu.VMEM((seq, HEAD_DIM), jnp.float32),))


def _mla_attn(q, k, v, batch, seq):
    specs = [_colblk(seq, MLA_QK_PAD, 0), _colblk(seq, MLA_QK_PAD, 0), _colblk(seq, HEAD_DIM, 0)]
    return _attn_call(_mla_attn_kernel, "mla_attn", batch, seq, MLA_HEADS, (q, k, v), specs)


def _fox_attn(z, ccol, crow, batch, seq):
    specs = []
    for col0 in (Z_FQ, Z_FK, Z_FV):
        specs += [_colblk(seq, LANES, col0 - FOX_SHIFT), _colblk(seq, LANES, col0 - FOX_SHIFT + LANES)]
    specs += [pl.BlockSpec((seq, LANES), lambda b, h: (b, 0)),
              pl.BlockSpec((1, SUBLANES, seq), lambda b, h: (b, 0, 0))]
    kv = pltpu.VMEM((seq, HEAD_DIM), jnp.bfloat16)
    return _attn_call(_fox_attn_kernel, "fox_attn", batch, seq, FOX_HEADS, (z,) * 6 + (ccol, crow), specs,
                      extra_scratch=(kv,))


def _out_proj_kernel(x_ref, a_ref, b_ref, c_ref, wb_ref, o_ref):
    na, nb = a_ref.shape[1], b_ref.shape[1]
    acc = jnp.dot(a_ref[...], wb_ref[:na, :], preferred_element_type=jnp.float32)
    acc = acc + jnp.dot(b_ref[...], wb_ref[na:na + nb, :], preferred_element_type=jnp.float32)
    acc = acc + jnp.dot(c_ref[...], wb_ref[na + nb:, :], preferred_element_type=jnp.float32)
    o_ref[...] = x_ref[...] + acc


def _out_proj(l, x, oa, ob, oc, w_o):
    m, d = x.shape
    dm = w_o.shape[1]
    assert oa.shape[1] + ob.shape[1] + oc.shape[1] == dm
    lhs = lambda a: pl.BlockSpec((TM_OUT, a.shape[1]), lambda i: (i, 0))
    xo = pl.BlockSpec((TM_OUT, d), lambda i: (i, 0))
    return pl.pallas_call(
        _out_proj_kernel,
        grid=(m // TM_OUT,),
        in_specs=[xo, lhs(oa), lhs(ob), lhs(oc), _layer_spec(l, (dm, d))],
        out_specs=xo,
        out_shape=jax.ShapeDtypeStruct((m, d), jnp.float32),
        compiler_params=_cparams(("parallel",)),
        name="out_proj",
    )(x, oa, ob, oc, w_o)


def _ffn_up_kernel(tiles_per_seq, x_ref, g_ref, wa_ref, wg_ref, cwa_ref, cwg_ref, cba_ref, cbg_ref,
                   o_ref, h_ref, ua_ref, ug_ref, carry_ref):
    i, j = pl.program_id(0), pl.program_id(1)
    tm = x_ref.shape[0]

    @pl.when(j == 0)
    def _():
        x = x_ref[...]
        h_ref[...] = (x * _rms_scale(x, NORM_EPS) * g_ref[...]).astype(jnp.bfloat16)

    @pl.when(i % tiles_per_seq == 0)
    def _():
        carry_ref[j] = jnp.zeros(carry_ref.shape[1:], jnp.float32)

    ua_ref[0:SUBLANES, :] = carry_ref[j, 0]
    ug_ref[0:SUBLANES, :] = carry_ref[j, 1]

    def conv(u_ref, w_ref, half, cw_ref, cb_ref):
        u = jnp.dot(h_ref[...], w_ref[...].astype(jnp.bfloat16), preferred_element_type=jnp.float32)
        u_ref[SUBLANES:, :] = u
        carry_ref[j, half] = u[tm - SUBLANES:, :]
        cw = cw_ref[...]
        return (u * cw[2:3] + u_ref[SUBLANES - 1:SUBLANES - 1 + tm, :] * cw[1:2]
                + u_ref[SUBLANES - 2:SUBLANES - 2 + tm, :] * cw[0:1] + cb_ref[...])

    g = conv(ug_ref, wg_ref, 1, cwg_ref, cbg_ref)
    gate = g / (1.0 + jnp.exp(-g))
    a = conv(ua_ref, wa_ref, 0, cwa_ref, cba_ref)
    o_ref[...] = (gate * a).astype(o_ref.dtype)


def _ffn_up(l, x, gain, w_up, conv_w, conv_b, seq):
    m, d = x.shape
    nj = D_FF // TN_FF
    kern = functools.partial(_ffn_up_kernel, seq // TM_PROJ)
    col = lambda rows, off: _layer_spec(l, (rows, TN_FF), lambda i, j: j + off)
    return pl.pallas_call(
        kern,
        grid=(m // TM_PROJ, nj),
        in_specs=[pl.BlockSpec((TM_PROJ, d), lambda i, j: (i, 0)),
                  _layer_spec(l, (1, d)),
                  col(d, 0), col(d, nj), col(3, 0), col(3, nj), col(1, 0), col(1, nj)],
        out_specs=pl.BlockSpec((TM_PROJ, TN_FF), lambda i, j: (i, j)),
        out_shape=jax.ShapeDtypeStruct((m, D_FF), jnp.bfloat16),
        scratch_shapes=[pltpu.VMEM((TM_PROJ, d), jnp.bfloat16),
                        pltpu.VMEM((TM_PROJ + SUBLANES, TN_FF), jnp.float32),
                        pltpu.VMEM((TM_PROJ + SUBLANES, TN_FF), jnp.float32),
                        pltpu.VMEM((nj, 2, SUBLANES, TN_FF), jnp.float32)],
        compiler_params=_cparams(("arbitrary", "arbitrary")),
        name="ffn_up",
    )(x, gain, w_up, w_up, conv_w, conv_w, conv_b, conv_b)


def _ffn_down_kernel(x_ref, a_ref, w_ref, o_ref):
    o_ref[...] = x_ref[...] + jnp.dot(a_ref[...], w_ref[...], preferred_element_type=jnp.float32)


def _ffn_down(l, x, act, w):
    m, d = x.shape
    k = act.shape[1]
    xo = pl.BlockSpec((TM_PROJ, TN_DOWN), lambda i, j: (i, j))
    return pl.pallas_call(
        _ffn_down_kernel,
        grid=(m // TM_PROJ, d // TN_DOWN),
        in_specs=[xo, pl.BlockSpec((TM_PROJ, k), lambda i, j: (i, 0)),
                  _layer_spec(l, (k, TN_DOWN), lambda i, j: j)],
        out_specs=xo,
        out_shape=jax.ShapeDtypeStruct((m, d), jnp.float32),
        compiler_params=_cparams(("parallel", "parallel")),
        name="ffn_down",
    )(x, act, w)


def _final_norm_kernel(x_ref, g_ref, o_ref):
    x = x_ref[...]
    o_ref[...] = x * _rms_scale(x, NORM_EPS) * g_ref[...]


def _final_norm(x, gain):
    m, d = x.shape
    blk = pl.BlockSpec((TM_PREP, d), lambda i: (i, 0))
    return pl.pallas_call(
        _final_norm_kernel,
        grid=(m // TM_PREP,),
        in_specs=[blk, pl.BlockSpec((1, d), lambda i: (0, 0))],
        out_specs=blk,
        out_shape=jax.ShapeDtypeStruct((m, d), jnp.float32),
        compiler_params=_cparams(("parallel",)),
        name="final_norm",
    )(x, gain)


def _w_in_prep_kernel(w_ref, t_ref, o_ref):
    c = pl.program_id(0)
    n_main = pl.num_programs(0) - t_ref.shape[2] // LANES

    @pl.when(c < n_main)
    def _():
        for l in range(o_ref.shape[0]):
            o_ref[l] = w_ref[:, l, :].T.astype(jnp.bfloat16)

    for t in range(t_ref.shape[2] // LANES):
        @pl.when(c == n_main + t)
        def _():
            o_ref[...] = t_ref[:, :, t * LANES:(t + 1) * LANES]


def _w_in_prep(w, width):
    dp, k, n = w.shape
    n_main = n // LANES
    tail_w = width - n_main * LANES
    assert width % LANES == 0 and tail_w >= n - n_main * LANES
    wt = jnp.transpose(w, (2, 0, 1))
    tail = jnp.pad(w[:, :, n_main * LANES:], ((0, 0), (0, 0), (0, tail_w - (n - n_main * LANES))))
    return pl.pallas_call(
        _w_in_prep_kernel,
        grid=(width // LANES,),
        in_specs=[pl.BlockSpec((LANES, dp, k), lambda c: (jnp.minimum(c, n_main - 1), 0, 0)),
                  pl.BlockSpec((dp, k, tail_w), lambda c: (0, 0, 0))],
        out_specs=pl.BlockSpec((dp, k, LANES), lambda c: (0, 0, c)),
        out_shape=jax.ShapeDtypeStruct((dp, k, width), jnp.bfloat16),
        compiler_params=_cparams(("arbitrary",)),
        name="w_in_prep",
    )(wt, tail.astype(jnp.bfloat16))


def _pad_w_uq(w):
    dp, r, _ = w.shape
    w = w.reshape(dp, r, MLA_HEADS, MLA_QK_NOPE + MLA_QK_ROPE)
    w = jnp.pad(w, ((0, 0), (0, 0), (0, 0), (0, MLA_QK_PAD - MLA_QK_NOPE - MLA_QK_ROPE)))
    return w.reshape(dp, r, MLA_HEADS * MLA_QK_PAD).astype(jnp.bfloat16)


def _split_w_ukv(w):
    dp, r, _ = w.shape
    w = w.reshape(dp, r, MLA_HEADS, MLA_QK_NOPE + HEAD_DIM)
    wk = w[..., :MLA_QK_NOPE].reshape(dp, r, MLA_HEADS * MLA_QK_NOPE)
    wv = w[..., MLA_QK_NOPE:].reshape(dp, r, MLA_HEADS * HEAD_DIM)
    return wk.astype(jnp.bfloat16), wv.astype(jnp.bfloat16)


def kernel(x, positions, attn_norm, w_in, diff_lambda, diff_out_norm, mla_q_norm, mla_kv_norm, mla_w_uq,
           mla_w_ukv, fox_forget_bias, w_o, ffn_norm, ffn_w_up, ffn_conv_w, ffn_conv_b, ffn_w_down,
           final_norm):
    batch, seq, d = x.shape
    depth = w_in.shape[0]
    m = batch * seq
    assert seq % TQ == 0 and seq % TM_PROJ == 0 and m % TM_PROJ == 0
    assert w_in.shape[2] == IN_WIDTH

    pat_a, pat_b = _rope_lane_patterns()
    pos_col = positions.reshape(m, 1)
    tabs_a = _rope_tables(pos_col, *pat_a)
    tabs_b = _rope_tables(pos_col, *pat_b)

    bf = lambda a: a.astype(jnp.bfloat16)
    rows = lambda a: a.reshape(depth, 1, -1)
    w_in_b = _w_in_prep(w_in, Z_WIDTH)
    w_uq_b = _pad_w_uq(mla_w_uq)
    w_uk_b, w_uv_b = _split_w_ukv(mla_w_ukv)
    w_o_b, w_down_b = bf(w_o), bf(ffn_w_down)
    fb = jnp.pad(fox_forget_bias, ((0, 0), (FOX_SHIFT, LANES - FOX_SHIFT - FOX_HEADS))).reshape(depth, 1, LANES)
    g_attn, g_ffn, g_q, g_kv, g_diff = (rows(attn_norm), rows(ffn_norm), rows(mla_q_norm),
                                        rows(mla_kv_norm), rows(diff_out_norm))
    conv_b = rows(ffn_conv_b)

    xf = x.reshape(m, d)
    for l in range(depth):
        lambda_init = 0.8 - 0.6 * math.exp(-0.3 * l)
        z, zg = _in_proj(l, xf, g_attn, w_in_b, tabs_a)
        q_b, k_b, v_b = _mla_up(l, z, g_q, g_kv, w_uq_b, w_uk_b, w_uv_b, tabs_b)
        ccol, crow = _fox_gate(l, zg, fb, batch, seq)
        o_a = _diff_attn(l, z, diff_lambda, g_diff, lambda_init, batch, seq)
        o_b = _mla_attn(q_b, k_b, v_b, batch, seq)
        o_c = _fox_attn(z, ccol, crow, batch, seq)
        xf = _out_proj(l, xf, o_a, o_b, o_c, w_o_b)
        act = _ffn_up(l, xf, g_ffn, ffn_w_up, ffn_conv_w, conv_b, seq)
        xf = _ffn_down(l, xf, act, w_down_b)
    return _final_norm(xf, final_norm.reshape(1, d)).reshape(batch, seq, d)
```

```python
import functools
import math

import numpy as np
import jax
import jax.numpy as jnp
from jax import lax
from jax.experimental import pallas as pl
from jax.experimental.pallas import tpu as pltpu

HEAD_DIM = 128
DIFF_HEADS = 4
DIFF_QK_DIM = 64
MLA_HEADS = 6
MLA_LORA = 512
MLA_QK_NOPE = 128
MLA_QK_ROPE = 64
FOX_HEADS = 6
D_FF = 5632
ROPE_THETA = 500000.0
PARTIAL_ROT_DIM = DIFF_QK_DIM // 4
NORM_EPS = 1e-6
SUBLN_EPS = 1e-5
LOG2E = math.log2(math.e)

LANES = 128
SUBLANES = 8
VMEM_LIMIT_BYTES = 56 * 1024 * 1024

Z_AQ, Z_AK, Z_AV = 0, 512, 1024
Z_CQ, Z_CKV, Z_KR = 1536, 2048, 2560
Z_FQ, Z_FK, Z_FV = 2624, 3392, 4160
Z_FG = 4928
IN_WIDTH = Z_FG + FOX_HEADS
FOX_SHIFT = Z_FQ % LANES
assert Z_FK % LANES == FOX_SHIFT and Z_FV % LANES == FOX_SHIFT and Z_FG % LANES == FOX_SHIFT
MLA_QK_PAD = 256

TM_PROJ = 1024
TN_IN = 1280
TM_OUT = 512
TN_FF = 512
TN_DOWN = 512
TM_PREP = 1024
TQ = 256
Z_WIDTH = -(-IN_WIDTH // TN_IN) * TN_IN


def _cparams(sem):
    return pltpu.CompilerParams(dimension_semantics=sem, vmem_limit_bytes=VMEM_LIMIT_BYTES)


def _rms_scale(x32, eps):
    return lax.rsqrt(jnp.mean(x32 * x32, axis=-1, keepdims=True) + eps)


def _layer_spec(l, shape, col=None):
    if col is None:
        return pl.BlockSpec((None,) + shape, lambda *g: (l, 0, 0))
    return pl.BlockSpec((None,) + shape, lambda *g: (l, 0, col(*g)))


def _rope_table_kernel(pos_ref, invf_ref, m1_ref, m2_ref, c_ref, s1_ref, s2_ref):
    ang = pos_ref[...].astype(jnp.float32) * invf_ref[...]
    sn = jnp.sin(ang)
    c_ref[...] = jnp.cos(ang)
    s1_ref[...] = sn * m1_ref[...]
    s2_ref[...] = sn * m2_ref[...]


def _rope_tables(pos_col, invf, m1, m2):
    m = pos_col.shape[0]
    row = pl.BlockSpec((1, LANES), lambda i: (0, 0))
    tab = pl.BlockSpec((TM_PREP, LANES), lambda i: (i, 0))
    return pl.pallas_call(
        _rope_table_kernel,
        grid=(m // TM_PREP,),
        in_specs=[pl.BlockSpec((TM_PREP, 1), lambda i: (i, 0)), row, row, row],
        out_specs=[tab, tab, tab],
        out_shape=[jax.ShapeDtypeStruct((m, LANES), jnp.float32)] * 3,
        compiler_params=_cparams(("parallel",)),
        name="rope_tables",
    )(pos_col, invf, m1, m2)


def _rope_lane_patterns():
    lane = np.arange(LANES)
    half_a = PARTIAL_ROT_DIM // 2
    r = lane % DIFF_QK_DIM
    inv_a = ROPE_THETA ** (-jnp.arange(0, PARTIAL_ROT_DIM, 2, dtype=jnp.float32) / PARTIAL_ROT_DIM)
    invf_a = jnp.where(jnp.asarray(r < PARTIAL_ROT_DIM), inv_a[jnp.asarray(r % half_a)], 0.0)
    m1_a = np.where(r < half_a, -1.0, 0.0)
    m2_a = np.where((r >= half_a) & (r < PARTIAL_ROT_DIM), 1.0, 0.0)
    half_b = MLA_QK_ROPE // 2
    inv_b = ROPE_THETA ** (-jnp.arange(0, MLA_QK_ROPE, 2, dtype=jnp.float32) / MLA_QK_ROPE)
    invf_b = jnp.where(jnp.asarray(lane < MLA_QK_ROPE), inv_b[jnp.asarray(lane % half_b)], 0.0)
    m1_b = np.where(lane < half_b, -1.0, 0.0)
    m2_b = np.where((lane >= half_b) & (lane < MLA_QK_ROPE), 1.0, 0.0)
    f = lambda a: jnp.asarray(a, jnp.float32).reshape(1, LANES)
    return (f(invf_a), f(m1_a), f(m2_a)), (f(invf_b), f(m1_b), f(m2_b))


def _rope_block(x, c, s1, s2, d):
    return x * c + pltpu.roll(x, LANES - d, 1) * s1 + pltpu.roll(x, d, 1) * s2


def _in_proj_kernel(x_ref, g_ref, w_ref, c_ref, s1_ref, s2_ref, z_ref, zg_ref, h_ref):
    j = pl.program_id(1)

    @pl.when(j == 0)
    def _():
        x = x_ref[...]
        h_ref[...] = (x * _rms_scale(x, NORM_EPS) * g_ref[...]).astype(jnp.bfloat16)

    @pl.when(j == 0)
    def _():
        c, s1, s2 = c_ref[...], s1_ref[...], s2_ref[...]
        scale = DIFF_QK_DIM ** -0.5 * LOG2E
        acc = jnp.dot(h_ref[...], w_ref[...], preferred_element_type=jnp.float32)
        for b in range(Z_AV // LANES):
            sl = slice(b * LANES, (b + 1) * LANES)
            r = _rope_block(acc[:, sl], c, s1, s2, PARTIAL_ROT_DIM // 2)
            z_ref[:, sl] = ((r * scale) if b * LANES < Z_AK else r).astype(jnp.bfloat16)
        z_ref[:, Z_AV:] = acc[:, Z_AV:].astype(jnp.bfloat16)

    @pl.when(j != 0)
    def _():
        acc = jnp.dot(h_ref[...], w_ref[...], preferred_element_type=jnp.float32)
        z_ref[...] = acc.astype(jnp.bfloat16)

        @pl.when(j == Z_FG // TN_IN)
        def _():
            off = Z_FG % TN_IN // LANES * LANES
            zg_ref[...] = acc[:, off:off + LANES]


def _in_proj(l, x, gain, w, tabs):
    m, d = x.shape
    assert Z_AQ == 0 and TN_IN >= Z_AV
    tab = pl.BlockSpec((TM_PROJ, LANES), lambda i, j: (i, 0))
    return pl.pallas_call(
        _in_proj_kernel,
        grid=(m // TM_PROJ, Z_WIDTH // TN_IN),
        in_specs=[pl.BlockSpec((TM_PROJ, d), lambda i, j: (i, 0)),
                  _layer_spec(l, (1, d)),
                  _layer_spec(l, (d, TN_IN), lambda i, j: j),
                  tab, tab, tab],
        out_specs=[pl.BlockSpec((TM_PROJ, TN_IN), lambda i, j: (i, j)),
                   pl.BlockSpec((TM_PROJ, LANES), lambda i, j: (i, 0))],
        out_shape=[jax.ShapeDtypeStruct((m, Z_WIDTH), jnp.bfloat16),
                   jax.ShapeDtypeStruct((m, LANES), jnp.float32)],
        scratch_shapes=[pltpu.VMEM((TM_PROJ, d), jnp.bfloat16)],
        compiler_params=_cparams(("parallel", "arbitrary")),
        name="in_proj",
    )(x, gain, w, *tabs)


def _mla_up_kernel(cq_ref, ckv_ref, kr_ref, gq_ref, gkv_ref, wq_ref, wk_ref, wv_ref,
                   c_ref, s1_ref, s2_ref, q_ref, k_ref, v_ref):
    c, s1, s2 = c_ref[...], s1_ref[...], s2_ref[...]
    half = MLA_QK_ROPE // 2
    scale = (MLA_QK_NOPE + MLA_QK_ROPE) ** -0.5 * LOG2E

    cq = cq_ref[...].astype(jnp.float32)
    cqn = (cq * _rms_scale(cq, NORM_EPS) * gq_ref[...]).astype(jnp.bfloat16)
    q = jnp.dot(cqn, wq_ref[...], preferred_element_type=jnp.float32)
    ckv = ckv_ref[...].astype(jnp.float32)
    ckvn = (ckv * _rms_scale(ckv, NORM_EPS) * gkv_ref[...]).astype(jnp.bfloat16)
    kn = jnp.dot(ckvn, wk_ref[...], preferred_element_type=jnp.float32)
    v_ref[...] = jnp.dot(ckvn, wv_ref[...], preferred_element_type=jnp.float32).astype(jnp.bfloat16)
    kr = _rope_block(kr_ref[...].astype(jnp.float32), c, s1, s2, half)
    lane = lax.broadcasted_iota(jnp.int32, kr.shape, 1)
    kr = jnp.where(lane < MLA_QK_ROPE, kr, 0.0).astype(jnp.bfloat16)

    for h in range(MLA_HEADS):
        base = h * MLA_QK_PAD
        q_ref[:, base:base + LANES] = (q[:, base:base + LANES] * scale).astype(jnp.bfloat16)
        qr = _rope_block(q[:, base + LANES:base + 2 * LANES], c, s1, s2, half)
        q_ref[:, base + LANES:base + 2 * LANES] = (qr * scale).astype(jnp.bfloat16)
        k_ref[:, base:base + LANES] = kn[:, h * LANES:(h + 1) * LANES].astype(jnp.bfloat16)
        k_ref[:, base + LANES:base + 2 * LANES] = kr


def _mla_up(l, z, gq, gkv, wq, wk, wv, tabs):
    m = z.shape[0]
    hq = MLA_HEADS * MLA_QK_PAD
    hv = MLA_HEADS * HEAD_DIM
    tab = pl.BlockSpec((TM_PREP, LANES), lambda i: (i, 0))
    return pl.pallas_call(
        _mla_up_kernel,
        grid=(m // TM_PREP,),
        in_specs=[pl.BlockSpec((TM_PREP, MLA_LORA), lambda i: (i, Z_CQ // MLA_LORA)),
                  pl.BlockSpec((TM_PREP, MLA_LORA), lambda i: (i, Z_CKV // MLA_LORA)),
                  pl.BlockSpec((TM_PREP, LANES), lambda i: (i, Z_KR // LANES)),
                  _layer_spec(l, (1, MLA_LORA)), _layer_spec(l, (1, MLA_LORA)),
                  _layer_spec(l, (MLA_LORA, hq)), _layer_spec(l, (MLA_LORA, hv)),
                  _layer_spec(l, (MLA_LORA, hv)),
                  tab, tab, tab],
        out_specs=[pl.BlockSpec((TM_PREP, hq), lambda i: (i, 0)),
                   pl.BlockSpec((TM_PREP, hq), lambda i: (i, 0)),
                   pl.BlockSpec((TM_PREP, hv), lambda i: (i, 0))],
        out_shape=[jax.ShapeDtypeStruct((m, hq), jnp.bfloat16),
                   jax.ShapeDtypeStruct((m, hq), jnp.bfloat16),
                   jax.ShapeDtypeStruct((m, hv), jnp.bfloat16)],
        compiler_params=_cparams(("parallel",)),
        name="mla_up",
    )(z, z, z, gq, gkv, wq, wk, wv, *tabs)


def _fox_gate_kernel(zg_ref, fb_ref, col_ref, row_ref):
    x = zg_ref[...] + fb_ref[...]
    logf = jnp.minimum(x, 0.0) - jnp.log(1.0 + jnp.exp(-jnp.abs(x)))
    s = logf.shape[0]
    row = lax.broadcasted_iota(jnp.int32, logf.shape, 0)
    d = 1
    while d < s:
        logf = logf + jnp.where(row >= d, pltpu.roll(logf, d, 0), 0.0)
        d *= 2
    logf = logf * LOG2E
    col_ref[...] = logf
    row_ref[0] = logf.T[FOX_SHIFT:FOX_SHIFT + SUBLANES, :]


def _fox_gate(l, zg, fb, batch, seq):
    return pl.pallas_call(
        _fox_gate_kernel,
        grid=(batch,),
        in_specs=[pl.BlockSpec((seq, LANES), lambda b: (b, 0)),
                  _layer_spec(l, (1, LANES))],
        out_specs=[pl.BlockSpec((seq, LANES), lambda b: (b, 0)),
                   pl.BlockSpec((1, SUBLANES, seq), lambda b: (b, 0, 0))],
        out_shape=[jax.ShapeDtypeStruct((batch * seq, LANES), jnp.float32),
                   jax.ShapeDtypeStruct((batch, SUBLANES, seq), jnp.float32)],
        compiler_params=_cparams(("parallel",)),
        name="fox_gate",
    )(zg, fb)


_NT = (((1,), (1,)), ((), ()))


def _scores(q, k_ref, s_ref, bias_fn=None):
    n = s_ref.shape[0] // TQ
    for c in range(n):
        cs = slice(c * TQ, (c + 1) * TQ)
        s = lax.dot_general(q[c * TQ:, :], k_ref[cs, :], _NT, preferred_element_type=jnp.float32)
        if bias_fn is not None:
            s = s + bias_fn(c)
        s_ref[c * TQ:, cs] = s


def _softmax_tile(s_ref, p_ref, i):
    rs = slice(i * TQ, (i + 1) * TQ)

    def chunk(c):
        s = s_ref[rs, c * TQ:(c + 1) * TQ]
        if c == i:
            rows = lax.broadcasted_iota(jnp.int32, s.shape, 0)
            cols = lax.broadcasted_iota(jnp.int32, s.shape, 1)
            s = jnp.where(cols <= rows, s, -jnp.inf)
        return s

    mpart = None
    for c in range(i + 1):
        s = chunk(c)
        for b in range(TQ // LANES):
            blk = s[:, b * LANES:(b + 1) * LANES]
            mpart = blk if mpart is None else jnp.maximum(mpart, blk)
    m = jnp.max(mpart, axis=1, keepdims=True)
    for c in range(i + 1):
        p_ref[rs, c * TQ:(c + 1) * TQ] = jnp.exp2(chunk(c) - m).astype(jnp.bfloat16)


def _fill_values(va_ref, v):
    va_ref[:, :HEAD_DIM] = v
    va_ref[:, HEAD_DIM:] = jnp.ones((va_ref.shape[0], va_ref.shape[1] - HEAD_DIM), va_ref.dtype)


def _attend(p_ref, va_ref, i):
    n = (i + 1) * TQ
    o = jnp.dot(p_ref[i * TQ:(i + 1) * TQ, :n], va_ref[:n, :], preferred_element_type=jnp.float32)
    return o[:, :HEAD_DIM] / o[:, HEAD_DIM:]


def _mla_attn_kernel(q_ref, k_ref, v_ref, o_ref, s_ref, p_ref, va_ref):
    _fill_values(va_ref, v_ref[...])
    _scores(q_ref[...], k_ref, s_ref)
    for i in range(q_ref.shape[0] // TQ):
        _softmax_tile(s_ref, p_ref, i)
        o_ref[i * TQ:(i + 1) * TQ, :] = _attend(p_ref, va_ref, i).astype(o_ref.dtype)


def _unshift(lo, hi):
    return jnp.concatenate([lo[:, FOX_SHIFT:], hi[:, :FOX_SHIFT]], axis=1)


def _fox_attn_kernel(qlo_ref, qhi_ref, klo_ref, khi_ref, vlo_ref, vhi_ref, ccol_ref, crow_ref,
                     o_ref, s_ref, p_ref, va_ref, k_ref):
    h = pl.program_id(1)
    scale = HEAD_DIM ** -0.5 * LOG2E
    k_ref[...] = _unshift(klo_ref[...], khi_ref[...])
    _fill_values(va_ref, _unshift(vlo_ref[...], vhi_ref[...]))
    q = _unshift(qlo_ref[...], qhi_ref[...])
    q = (q.astype(jnp.float32) * scale).astype(jnp.bfloat16)
    crow = crow_ref[0, pl.ds(h, 1), :]
    cc = ccol_ref[...]
    lane = lax.broadcasted_iota(jnp.int32, cc.shape, 1)
    ccq = jnp.sum(jnp.where(lane == h + FOX_SHIFT, cc, 0.0), axis=1, keepdims=True)
    _scores(q, k_ref, s_ref, bias_fn=lambda c: ccq[c * TQ:, :] - crow[:, c * TQ:(c + 1) * TQ])
    for i in range(o_ref.shape[0] // TQ):
        _softmax_tile(s_ref, p_ref, i)
        o_ref[i * TQ:(i + 1) * TQ, :] = _attend(p_ref, va_ref, i).astype(o_ref.dtype)


def _diff_attn_kernel(lambda_init, q_ref, k_ref, v_ref, lam_ref, g_ref, o_ref, s_ref, p_ref, va_ref, o1_ref):
    lv = lam_ref[...]
    lam = (jnp.exp(jnp.sum(lv[0:1] * lv[1:2], axis=1, keepdims=True))
           - jnp.exp(jnp.sum(lv[2:3] * lv[3:4], axis=1, keepdims=True)) + lambda_init)
    gain = g_ref[...] * (1.0 - lambda_init)
    q = q_ref[...]
    lane = lax.broadcasted_iota(jnp.int32, q.shape, 1)
    zero = jnp.zeros_like(q)
    nq = q_ref.shape[0] // TQ
    _fill_values(va_ref, v_ref[...])
    _scores(jnp.where(lane < DIFF_QK_DIM, q, zero), k_ref, s_ref)
    for i in range(nq):
        _softmax_tile(s_ref, p_ref, i)
        o1_ref[i * TQ:(i + 1) * TQ, :] = _attend(p_ref, va_ref, i)
    _scores(jnp.where(lane >= DIFF_QK_DIM, q, zero), k_ref, s_ref)
    for i in range(nq):
        rs = slice(i * TQ, (i + 1) * TQ)
        _softmax_tile(s_ref, p_ref, i)
        o = o1_ref[rs, :] - lam * _attend(p_ref, va_ref, i)
        o_ref[rs, :] = (o * _rms_scale(o, SUBLN_EPS) * gain).astype(o_ref.dtype)


def _attn_call(kernel, name, batch, seq, heads, ins, in_specs, extra_scratch=()):
    return pl.pallas_call(
        kernel,
        grid=(batch, heads),
        in_specs=in_specs,
        out_specs=pl.BlockSpec((seq, HEAD_DIM), lambda b, h: (b, h)),
        out_shape=jax.ShapeDtypeStruct((batch * seq, heads * HEAD_DIM), jnp.bfloat16),
        scratch_shapes=[pltpu.VMEM((seq, seq), jnp.float32), pltpu.VMEM((seq, seq), jnp.bfloat16),
                        pltpu.VMEM((seq, 2 * HEAD_DIM), jnp.bfloat16), *extra_scratch],
        compiler_params=_cparams(("parallel", "parallel")),
        name=name,
    )(*ins)


def _colblk(seq, width, col0):
    assert col0 % width == 0
    return pl.BlockSpec((seq, width), lambda b, h: (b, col0 // width + h))


def _diff_attn(l, z, lam, gain, lambda_init, batch, seq):
    kern = functools.partial(_diff_attn_kernel, lambda_init)
    specs = [_colblk(seq, HEAD_DIM, Z_AQ), _colblk(seq, HEAD_DIM, Z_AK), _colblk(seq, HEAD_DIM, Z_AV),
             _layer_spec(l, (4, DIFF_QK_DIM)), _layer_spec(l, (1, HEAD_DIM))]
    return _attn_call(kern, "diff_attn", batch, seq, DIFF_HEADS, (z, z, z, lam, gain), specs,
                      extra_scratch=(pltpu.VMEM((seq, HEAD_DIM), jnp.float32),))


def _mla_attn(q, k, v, batch, seq):
    specs = [_colblk(seq, MLA_QK_PAD, 0), _colblk(seq, MLA_QK_PAD, 0), _colblk(seq, HEAD_DIM, 0)]
    return _attn_call(_mla_attn_kernel, "mla_attn", batch, seq, MLA_HEADS, (q, k, v), specs)


def _fox_attn(z, ccol, crow, batch, seq):
    specs = []
    for col0 in (Z_FQ, Z_FK, Z_FV):
        specs += [_colblk(seq, LANES, col0 - FOX_SHIFT), _colblk(seq, LANES, col0 - FOX_SHIFT + LANES)]
    specs += [pl.BlockSpec((seq, LANES), lambda b, h: (b, 0)),
              pl.BlockSpec((1, SUBLANES, seq), lambda b, h: (b, 0, 0))]
    kv = pltpu.VMEM((seq, HEAD_DIM), jnp.bfloat16)
    return _attn_call(_fox_attn_kernel, "fox_attn", batch, seq, FOX_HEADS, (z,) * 6 + (ccol, crow), specs,
                      extra_scratch=(kv,))


def _out_proj_kernel(x_ref, a_ref, b_ref, c_ref, w_ref, o_ref, wb_ref):
    @pl.when(pl.program_id(0) == 0)
    def _():
        wb_ref[...] = w_ref[...].astype(jnp.bfloat16)

    na, nb = a_ref.shape[1], b_ref.shape[1]
    acc = jnp.dot(a_ref[...], wb_ref[:na, :], preferred_element_type=jnp.float32)
    acc = acc + jnp.dot(b_ref[...], wb_ref[na:na + nb, :], preferred_element_type=jnp.float32)
    acc = acc + jnp.dot(c_ref[...], wb_ref[na + nb:, :], preferred_element_type=jnp.float32)
    o_ref[...] = x_ref[...] + acc


def _out_proj(l, x, oa, ob, oc, w_o):
    m, d = x.shape
    dm = w_o.shape[1]
    assert oa.shape[1] + ob.shape[1] + oc.shape[1] == dm
    lhs = lambda a: pl.BlockSpec((TM_OUT, a.shape[1]), lambda i: (i, 0))
    xo = pl.BlockSpec((TM_OUT, d), lambda i: (i, 0))
    return pl.pallas_call(
        _out_proj_kernel,
        grid=(m // TM_OUT,),
        in_specs=[xo, lhs(oa), lhs(ob), lhs(oc),
                  pl.BlockSpec((None, dm, d), lambda i: (l, 0, 0), pipeline_mode=pl.Buffered(1))],
        out_specs=xo,
        out_shape=jax.ShapeDtypeStruct((m, d), jnp.float32),
        scratch_shapes=[pltpu.VMEM((dm, d), jnp.bfloat16)],
        compiler_params=_cparams(("arbitrary",)),
        name="out_proj",
    )(x, oa, ob, oc, w_o)


def _ffn_up_kernel(tiles_per_seq, x_ref, g_ref, wa_ref, wg_ref, cwa_ref, cwg_ref, cba_ref, cbg_ref,
                   o_ref, h_ref, ua_ref, ug_ref, carry_ref):
    i, j = pl.program_id(0), pl.program_id(1)
    tm = x_ref.shape[0]

    @pl.when(j == 0)
    def _():
        x = x_ref[...]
        h_ref[...] = (x * _rms_scale(x, NORM_EPS) * g_ref[...]).astype(jnp.bfloat16)

    @pl.when(i % tiles_per_seq == 0)
    def _():
        carry_ref[j] = jnp.zeros(carry_ref.shape[1:], jnp.float32)

    ua_ref[0:SUBLANES, :] = carry_ref[j, 0]
    ug_ref[0:SUBLANES, :] = carry_ref[j, 1]

    def conv(u_ref, w_ref, half, cw_ref, cb_ref):
        u = jnp.dot(h_ref[...], w_ref[...].astype(jnp.bfloat16), preferred_element_type=jnp.float32)
        u_ref[SUBLANES:, :] = u
        carry_ref[j, half] = u[tm - SUBLANES:, :]
        cw = cw_ref[...]
        return (u * cw[2:3] + u_ref[SUBLANES - 1:SUBLANES - 1 + tm, :] * cw[1:2]
                + u_ref[SUBLANES - 2:SUBLANES - 2 + tm, :] * cw[0:1] + cb_ref[...])

    g = conv(ug_ref, wg_ref, 1, cwg_ref, cbg_ref)
    gate = g / (1.0 + jnp.exp(-g))
    a = conv(ua_ref, wa_ref, 0, cwa_ref, cba_ref)
    o_ref[...] = (gate * a).astype(o_ref.dtype)


def _ffn_up(l, x, gain, w_up, conv_w, conv_b, seq):
    m, d = x.shape
    nj = D_FF // TN_FF
    kern = functools.partial(_ffn_up_kernel, seq // TM_PROJ)
    col = lambda rows, off: _layer_spec(l, (rows, TN_FF), lambda i, j: j + off)
    return pl.pallas_call(
        kern,
        grid=(m // TM_PROJ, nj),
        in_specs=[pl.BlockSpec((TM_PROJ, d), lambda i, j: (i, 0)),
                  _layer_spec(l, (1, d)),
                  col(d, 0), col(d, nj), col(3, 0), col(3, nj), col(1, 0), col(1, nj)],
        out_specs=pl.BlockSpec((TM_PROJ, TN_FF), lambda i, j: (i, j)),
        out_shape=jax.ShapeDtypeStruct((m, D_FF), jnp.bfloat16),
        scratch_shapes=[pltpu.VMEM((TM_PROJ, d), jnp.bfloat16),
                        pltpu.VMEM((TM_PROJ + SUBLANES, TN_FF), jnp.float32),
                        pltpu.VMEM((TM_PROJ + SUBLANES, TN_FF), jnp.float32),
                        pltpu.VMEM((nj, 2, SUBLANES, TN_FF), jnp.float32)],
        compiler_params=_cparams(("arbitrary", "arbitrary")),
        name="ffn_up",
    )(x, gain, w_up, w_up, conv_w, conv_w, conv_b, conv_b)


def _ffn_down_kernel(x_ref, a_ref, w_ref, o_ref):
    o_ref[...] = x_ref[...] + jnp.dot(a_ref[...], w_ref[...], preferred_element_type=jnp.float32)


def _ffn_down(l, x, act, w):
    m, d = x.shape
    k = act.shape[1]
    xo = pl.BlockSpec((TM_PROJ, TN_DOWN), lambda i, j: (i, j))
    return pl.pallas_call(
        _ffn_down_kernel,
        grid=(m // TM_PROJ, d // TN_DOWN),
        in_specs=[xo, pl.BlockSpec((TM_PROJ, k), lambda i, j: (i, 0)),
                  _layer_spec(l, (k, TN_DOWN), lambda i, j: j)],
        out_specs=xo,
        out_shape=jax.ShapeDtypeStruct((m, d), jnp.float32),
        compiler_params=_cparams(("parallel", "parallel")),
        name="ffn_down",
    )(x, act, w)


def _final_norm_kernel(x_ref, g_ref, o_ref):
    x = x_ref[...]
    o_ref[...] = x * _rms_scale(x, NORM_EPS) * g_ref[...]


def _final_norm(x, gain):
    m, d = x.shape
    blk = pl.BlockSpec((TM_PREP, d), lambda i: (i, 0))
    return pl.pallas_call(
        _final_norm_kernel,
        grid=(m // TM_PREP,),
        in_specs=[blk, pl.BlockSpec((1, d), lambda i: (0, 0))],
        out_specs=blk,
        out_shape=jax.ShapeDtypeStruct((m, d), jnp.float32),
        compiler_params=_cparams(("parallel",)),
        name="final_norm",
    )(x, gain)


def _w_in_prep_kernel(w_ref, t_ref, o_ref):
    c = pl.program_id(0)
    n_main = pl.num_programs(0) - t_ref.shape[2] // LANES

    @pl.when(c < n_main)
    def _():
        for l in range(o_ref.shape[0]):
            o_ref[l] = w_ref[:, l, :].T.astype(jnp.bfloat16)

    for t in range(t_ref.shape[2] // LANES):
        @pl.when(c == n_main + t)
        def _():
            o_ref[...] = t_ref[:, :, t * LANES:(t + 1) * LANES]


def _w_in_prep(w, width):
    dp, k, n = w.shape
    n_main = n // LANES
    tail_w = width - n_main * LANES
    assert width % LANES == 0 and tail_w >= n - n_main * LANES
    wt = jnp.transpose(w, (2, 0, 1))
    tail = jnp.pad(w[:, :, n_main * LANES:], ((0, 0), (0, 0), (0, tail_w - (n - n_main * LANES))))
    return pl.pallas_call(
        _w_in_prep_kernel,
        grid=(width // LANES,),
        in_specs=[pl.BlockSpec((LANES, dp, k), lambda c: (jnp.minimum(c, n_main - 1), 0, 0)),
                  pl.BlockSpec((dp, k, tail_w), lambda c: (0, 0, 0))],
        out_specs=pl.BlockSpec((dp, k, LANES), lambda c: (0, 0, c)),
        out_shape=jax.ShapeDtypeStruct((dp, k, width), jnp.bfloat16),
        compiler_params=_cparams(("arbitrary",)),
        name="w_in_prep",
    )(wt, tail.astype(jnp.bfloat16))


def _pad_w_uq(w):
    dp, r, _ = w.shape
    w = w.reshape(dp, r, MLA_HEADS, MLA_QK_NOPE + MLA_QK_ROPE)
    w = jnp.pad(w, ((0, 0), (0, 0), (0, 0), (0, MLA_QK_PAD - MLA_QK_NOPE - MLA_QK_ROPE)))
    return w.reshape(dp, r, MLA_HEADS * MLA_QK_PAD).astype(jnp.bfloat16)


def _split_w_ukv(w):
    dp, r, _ = w.shape
    w = w.reshape(dp, r, MLA_HEADS, MLA_QK_NOPE + HEAD_DIM)
    wk = w[..., :MLA_QK_NOPE].reshape(dp, r, MLA_HEADS * MLA_QK_NOPE)
    wv = w[..., MLA_QK_NOPE:].reshape(dp, r, MLA_HEADS * HEAD_DIM)
    return wk.astype(jnp.bfloat16), wv.astype(jnp.bfloat16)


def kernel(x, positions, attn_norm, w_in, diff_lambda, diff_out_norm, mla_q_norm, mla_kv_norm, mla_w_uq,
           mla_w_ukv, fox_forget_bias, w_o, ffn_norm, ffn_w_up, ffn_conv_w, ffn_conv_b, ffn_w_down,
           final_norm):
    batch, seq, d = x.shape
    depth = w_in.shape[0]
    m = batch * seq
    assert seq % TQ == 0 and seq % TM_PROJ == 0 and m % TM_PROJ == 0
    assert w_in.shape[2] == IN_WIDTH

    pat_a, pat_b = _rope_lane_patterns()
    pos_col = positions.reshape(m, 1)
    tabs_a = _rope_tables(pos_col, *pat_a)
    tabs_b = _rope_tables(pos_col, *pat_b)

    bf = lambda a: a.astype(jnp.bfloat16)
    rows = lambda a: a.reshape(depth, 1, -1)
    w_in_b = _w_in_prep(w_in, Z_WIDTH)
    w_uq_b = _pad_w_uq(mla_w_uq)
    w_uk_b, w_uv_b = _split_w_ukv(mla_w_ukv)
    w_down_b = bf(ffn_w_down)
    fb = jnp.pad(fox_forget_bias, ((0, 0), (FOX_SHIFT, LANES - FOX_SHIFT - FOX_HEADS))).reshape(depth, 1, LANES)
    g_attn, g_ffn, g_q, g_kv, g_diff = (rows(attn_norm), rows(ffn_norm), rows(mla_q_norm),
                                        rows(mla_kv_norm), rows(diff_out_norm))
    conv_b = rows(ffn_conv_b)

    xf = x.reshape(m, d)
    for l in range(depth):
        lambda_init = 0.8 - 0.6 * math.exp(-0.3 * l)
        z, zg = _in_proj(l, xf, g_attn, w_in_b, tabs_a)
        q_b, k_b, v_b = _mla_up(l, z, g_q, g_kv, w_uq_b, w_uk_b, w_uv_b, tabs_b)
        ccol, crow = _fox_gate(l, zg, fb, batch, seq)
        o_a = _diff_attn(l, z, diff_lambda, g_diff, lambda_init, batch, seq)
        o_b = _mla_attn(q_b, k_b, v_b, batch, seq)
        o_c = _fox_attn(z, ccol, crow, batch, seq)
        xf = _out_proj(l, xf, o_a, o_b, o_c, w_o)
        act = _ffn_up(l, xf, g_ffn, ffn_w_up, ffn_conv_w, conv_b, seq)
        xf = _ffn_down(l, xf, act, w_down_b)
    return _final_norm(xf, final_norm.reshape(1, d)).reshape(batch, seq, d)
```

```python
import functools
import math

import numpy as np
import jax
import jax.numpy as jnp
from jax import lax
from jax.experimental import pallas as pl
from jax.experimental.pallas import tpu as pltpu

HEAD_DIM = 128
DIFF_HEADS = 4
DIFF_QK_DIM = 64
MLA_HEADS = 6
MLA_LORA = 512
MLA_QK_NOPE = 128
MLA_QK_ROPE = 64
FOX_HEADS = 6
D_FF = 5632
ROPE_THETA = 500000.0
PARTIAL_ROT_DIM = DIFF_QK_DIM // 4
NORM_EPS = 1e-6
SUBLN_EPS = 1e-5
LOG2E = math.log2(math.e)

LANES = 128
SUBLANES = 8
VMEM_LIMIT_BYTES = 56 * 1024 * 1024

Z_AQ, Z_AK, Z_AV = 0, 512, 1024
Z_CQ, Z_CKV, Z_KR = 1536, 2048, 2560
Z_FQ, Z_FK, Z_FV = 2624, 3392, 4160
Z_FG = 4928
IN_WIDTH = Z_FG + FOX_HEADS
FOX_SHIFT = Z_FQ % LANES
assert Z_FK % LANES == FOX_SHIFT and Z_FV % LANES == FOX_SHIFT and Z_FG % LANES == FOX_SHIFT
MLA_QK_PAD = 256

TM_PROJ = 1024
TN_IN = 1280
TM_OUT = 512
TN_FF = 512
TN_DOWN = 512
TM_PREP = 1024
TQ = 256
Z_WIDTH = -(-IN_WIDTH // TN_IN) * TN_IN


def _cparams(sem):
    return pltpu.CompilerParams(dimension_semantics=sem, vmem_limit_bytes=VMEM_LIMIT_BYTES)


def _rms_scale(x32, eps):
    return lax.rsqrt(jnp.mean(x32 * x32, axis=-1, keepdims=True) + eps)


def _layer_spec(l, shape, col=None):
    if col is None:
        return pl.BlockSpec((None,) + shape, lambda *g: (l, 0, 0))
    return pl.BlockSpec((None,) + shape, lambda *g: (l, 0, col(*g)))


def _rope_table_kernel(pos_ref, invf_ref, m1_ref, m2_ref, c_ref, s1_ref, s2_ref):
    ang = pos_ref[...].astype(jnp.float32) * invf_ref[...]
    sn = jnp.sin(ang)
    c_ref[...] = jnp.cos(ang)
    s1_ref[...] = sn * m1_ref[...]
    s2_ref[...] = sn * m2_ref[...]


def _rope_tables(pos_col, invf, m1, m2):
    m = pos_col.shape[0]
    row = pl.BlockSpec((1, LANES), lambda i: (0, 0))
    tab = pl.BlockSpec((TM_PREP, LANES), lambda i: (i, 0))
    return pl.pallas_call(
        _rope_table_kernel,
        grid=(m // TM_PREP,),
        in_specs=[pl.BlockSpec((TM_PREP, 1), lambda i: (i, 0)), row, row, row],
        out_specs=[tab, tab, tab],
        out_shape=[jax.ShapeDtypeStruct((m, LANES), jnp.float32)] * 3,
        compiler_params=_cparams(("parallel",)),
        name="rope_tables",
    )(pos_col, invf, m1, m2)


def _rope_lane_patterns():
    lane = np.arange(LANES)
    half_a = PARTIAL_ROT_DIM // 2
    r = lane % DIFF_QK_DIM
    inv_a = ROPE_THETA ** (-jnp.arange(0, PARTIAL_ROT_DIM, 2, dtype=jnp.float32) / PARTIAL_ROT_DIM)
    invf_a = jnp.where(jnp.asarray(r < PARTIAL_ROT_DIM), inv_a[jnp.asarray(r % half_a)], 0.0)
    m1_a = np.where(r < half_a, -1.0, 0.0)
    m2_a = np.where((r >= half_a) & (r < PARTIAL_ROT_DIM), 1.0, 0.0)
    half_b = MLA_QK_ROPE // 2
    inv_b = ROPE_THETA ** (-jnp.arange(0, MLA_QK_ROPE, 2, dtype=jnp.float32) / MLA_QK_ROPE)
    invf_b = jnp.where(jnp.asarray(lane < MLA_QK_ROPE), inv_b[jnp.asarray(lane % half_b)], 0.0)
    m1_b = np.where(lane < half_b, -1.0, 0.0)
    m2_b = np.where((lane >= half_b) & (lane < MLA_QK_ROPE), 1.0, 0.0)
    f = lambda a: jnp.asarray(a, jnp.float32).reshape(1, LANES)
    return (f(invf_a), f(m1_a), f(m2_a)), (f(invf_b), f(m1_b), f(m2_b))


def _rope_block(x, c, s1, s2, d):
    return x * c + pltpu.roll(x, LANES - d, 1) * s1 + pltpu.roll(x, d, 1) * s2


def _in_proj_kernel(x_ref, g_ref, w_ref, c_ref, s1_ref, s2_ref, z_ref, zg_ref, h_ref):
    j = pl.program_id(1)

    @pl.when(j == 0)
    def _():
        x = x_ref[...]
        h_ref[...] = (x * _rms_scale(x, NORM_EPS) * g_ref[...]).astype(jnp.bfloat16)

    @pl.when(j == 0)
    def _():
        c, s1, s2 = c_ref[...], s1_ref[...], s2_ref[...]
        scale = DIFF_QK_DIM ** -0.5 * LOG2E
        acc = jnp.dot(h_ref[...], w_ref[...], preferred_element_type=jnp.float32)
        for b in range(Z_AV // LANES):
            sl = slice(b * LANES, (b + 1) * LANES)
            r = _rope_block(acc[:, sl], c, s1, s2, PARTIAL_ROT_DIM // 2)
            z_ref[:, sl] = ((r * scale) if b * LANES < Z_AK else r).astype(jnp.bfloat16)
        z_ref[:, Z_AV:] = acc[:, Z_AV:].astype(jnp.bfloat16)

    @pl.when(j != 0)
    def _():
        acc = jnp.dot(h_ref[...], w_ref[...], preferred_element_type=jnp.float32)
        z_ref[...] = acc.astype(jnp.bfloat16)

        @pl.when(j == Z_FG // TN_IN)
        def _():
            off = Z_FG % TN_IN // LANES * LANES
            zg_ref[...] = acc[:, off:off + LANES]


def _in_proj(l, x, gain, w, tabs):
    m, d = x.shape
    assert Z_AQ == 0 and TN_IN >= Z_AV
    tab = pl.BlockSpec((TM_PROJ, LANES), lambda i, j: (i, 0))
    return pl.pallas_call(
        _in_proj_kernel,
        grid=(m // TM_PROJ, Z_WIDTH // TN_IN),
        in_specs=[pl.BlockSpec((TM_PROJ, d), lambda i, j: (i, 0)),
                  _layer_spec(l, (1, d)),
                  _layer_spec(l, (d, TN_IN), lambda i, j: j),
                  tab, tab, tab],
        out_specs=[pl.BlockSpec((TM_PROJ, TN_IN), lambda i, j: (i, j)),
                   pl.BlockSpec((TM_PROJ, LANES), lambda i, j: (i, 0))],
        out_shape=[jax.ShapeDtypeStruct((m, Z_WIDTH), jnp.bfloat16),
                   jax.ShapeDtypeStruct((m, LANES), jnp.float32)],
        scratch_shapes=[pltpu.VMEM((TM_PROJ, d), jnp.bfloat16)],
        compiler_params=_cparams(("parallel", "arbitrary")),
        name="in_proj",
    )(x, gain, w, *tabs)


def _mla_up_kernel(cq_ref, ckv_ref, kr_ref, gq_ref, gkv_ref, wq_ref, wk_ref, wv_ref,
                   c_ref, s1_ref, s2_ref, q_ref, k_ref, v_ref):
    c, s1, s2 = c_ref[...], s1_ref[...], s2_ref[...]
    half = MLA_QK_ROPE // 2
    scale = (MLA_QK_NOPE + MLA_QK_ROPE) ** -0.5 * LOG2E

    cq = cq_ref[...].astype(jnp.float32)
    cqn = (cq * _rms_scale(cq, NORM_EPS) * gq_ref[...]).astype(jnp.bfloat16)
    q = jnp.dot(cqn, wq_ref[...], preferred_element_type=jnp.float32)
    ckv = ckv_ref[...].astype(jnp.float32)
    ckvn = (ckv * _rms_scale(ckv, NORM_EPS) * gkv_ref[...]).astype(jnp.bfloat16)
    kn = jnp.dot(ckvn, wk_ref[...], preferred_element_type=jnp.float32)
    v_ref[...] = jnp.dot(ckvn, wv_ref[...], preferred_element_type=jnp.float32).astype(jnp.bfloat16)
    kr = _rope_block(kr_ref[...].astype(jnp.float32), c, s1, s2, half)
    lane = lax.broadcasted_iota(jnp.int32, kr.shape, 1)
    kr = jnp.where(lane < MLA_QK_ROPE, kr, 0.0).astype(jnp.bfloat16)

    for h in range(MLA_HEADS):
        base = h * MLA_QK_PAD
        q_ref[:, base:base + LANES] = (q[:, base:base + LANES] * scale).astype(jnp.bfloat16)
        qr = _rope_block(q[:, base + LANES:base + 2 * LANES], c, s1, s2, half)
        q_ref[:, base + LANES:base + 2 * LANES] = (qr * scale).astype(jnp.bfloat16)
        k_ref[:, base:base + LANES] = kn[:, h * LANES:(h + 1) * LANES].astype(jnp.bfloat16)
        k_ref[:, base + LANES:base + 2 * LANES] = kr


def _mla_up(l, z, gq, gkv, wq, wk, wv, tabs):
    m = z.shape[0]
    hq = MLA_HEADS * MLA_QK_PAD
    hv = MLA_HEADS * HEAD_DIM
    tab = pl.BlockSpec((TM_PREP, LANES), lambda i: (i, 0))
    return pl.pallas_call(
        _mla_up_kernel,
        grid=(m // TM_PREP,),
        in_specs=[pl.BlockSpec((TM_PREP, MLA_LORA), lambda i: (i, Z_CQ // MLA_LORA)),
                  pl.BlockSpec((TM_PREP, MLA_LORA), lambda i: (i, Z_CKV // MLA_LORA)),
                  pl.BlockSpec((TM_PREP, LANES), lambda i: (i, Z_KR // LANES)),
                  _layer_spec(l, (1, MLA_LORA)), _layer_spec(l, (1, MLA_LORA)),
                  _layer_spec(l, (MLA_LORA, hq)), _layer_spec(l, (MLA_LORA, hv)),
                  _layer_spec(l, (MLA_LORA, hv)),
                  tab, tab, tab],
        out_specs=[pl.BlockSpec((TM_PREP, hq), lambda i: (i, 0)),
                   pl.BlockSpec((TM_PREP, hq), lambda i: (i, 0)),
                   pl.BlockSpec((TM_PREP, hv), lambda i: (i, 0))],
        out_shape=[jax.ShapeDtypeStruct((m, hq), jnp.bfloat16),
                   jax.ShapeDtypeStruct((m, hq), jnp.bfloat16),
                   jax.ShapeDtypeStruct((m, hv), jnp.bfloat16)],
        compiler_params=_cparams(("parallel",)),
        name="mla_up",
    )(z, z, z, gq, gkv, wq, wk, wv, *tabs)


def _fox_gate_kernel(zg_ref, fb_ref, col_ref, row_ref):
    x = zg_ref[...] + fb_ref[...]
    logf = jnp.minimum(x, 0.0) - jnp.log(1.0 + jnp.exp(-jnp.abs(x)))
    s = logf.shape[0]
    row = lax.broadcasted_iota(jnp.int32, logf.shape, 0)
    d = 1
    while d < s:
        logf = logf + jnp.where(row >= d, pltpu.roll(logf, d, 0), 0.0)
        d *= 2
    logf = logf * LOG2E
    col_ref[...] = logf
    row_ref[0] = logf.T[FOX_SHIFT:FOX_SHIFT + SUBLANES, :]


def _fox_gate(l, zg, fb, batch, seq):
    return pl.pallas_call(
        _fox_gate_kernel,
        grid=(batch,),
        in_specs=[pl.BlockSpec((seq, LANES), lambda b: (b, 0)),
                  _layer_spec(l, (1, LANES))],
        out_specs=[pl.BlockSpec((seq, LANES), lambda b: (b, 0)),
                   pl.BlockSpec((1, SUBLANES, seq), lambda b: (b, 0, 0))],
        out_shape=[jax.ShapeDtypeStruct((batch * seq, LANES), jnp.float32),
                   jax.ShapeDtypeStruct((batch, SUBLANES, seq), jnp.float32)],
        compiler_params=_cparams(("parallel",)),
        name="fox_gate",
    )(zg, fb)


_NT = (((1,), (1,)), ((), ()))


def _scores(q, k_ref, s_ref, bias_fn=None):
    n = s_ref.shape[0] // TQ
    for c in range(n):
        cs = slice(c * TQ, (c + 1) * TQ)
        s = lax.dot_general(q[c * TQ:, :], k_ref[cs, :], _NT, preferred_element_type=jnp.float32)
        if bias_fn is not None:
            s = s + bias_fn(c)
        s_ref[c * TQ:, cs] = s


def _softmax_tile(s_ref, p_ref, i):
    rs = slice(i * TQ, (i + 1) * TQ)

    def chunk(c):
        s = s_ref[rs, c * TQ:(c + 1) * TQ]
        if c == i:
            rows = lax.broadcasted_iota(jnp.int32, s.shape, 0)
            cols = lax.broadcasted_iota(jnp.int32, s.shape, 1)
            s = jnp.where(cols <= rows, s, -jnp.inf)
        return s

    mpart = None
    for c in range(i + 1):
        s = chunk(c)
        for b in range(TQ // LANES):
            blk = s[:, b * LANES:(b + 1) * LANES]
            mpart = blk if mpart is None else jnp.maximum(mpart, blk)
    m = jnp.max(mpart, axis=1, keepdims=True)
    for c in range(i + 1):
        p_ref[rs, c * TQ:(c + 1) * TQ] = jnp.exp2(chunk(c) - m).astype(jnp.bfloat16)


def _fill_values(va_ref, v):
    va_ref[:, :HEAD_DIM] = v
    va_ref[:, HEAD_DIM:] = jnp.ones((va_ref.shape[0], va_ref.shape[1] - HEAD_DIM), va_ref.dtype)


def _attend(p_ref, va_ref, i):
    n = (i + 1) * TQ
    o = jnp.dot(p_ref[i * TQ:(i + 1) * TQ, :n], va_ref[:n, :], preferred_element_type=jnp.float32)
    return o[:, :HEAD_DIM] / o[:, HEAD_DIM:]


def _mla_attn_kernel(q_ref, k_ref, v_ref, o_ref, s_ref, p_ref, va_ref):
    _fill_values(va_ref, v_ref[...])
    _scores(q_ref[...], k_ref, s_ref)
    for i in range(q_ref.shape[0] // TQ):
        _softmax_tile(s_ref, p_ref, i)
        o_ref[i * TQ:(i + 1) * TQ, :] = _attend(p_ref, va_ref, i).astype(o_ref.dtype)


def _unshift(lo, hi):
    return jnp.concatenate([lo[:, FOX_SHIFT:], hi[:, :FOX_SHIFT]], axis=1)


def _fox_attn_kernel(qlo_ref, qhi_ref, klo_ref, khi_ref, vlo_ref, vhi_ref, ccol_ref, crow_ref,
                     o_ref, s_ref, p_ref, va_ref, k_ref):
    h = pl.program_id(1)
    scale = HEAD_DIM ** -0.5 * LOG2E
    k_ref[...] = _unshift(klo_ref[...], khi_ref[...])
    _fill_values(va_ref, _unshift(vlo_ref[...], vhi_ref[...]))
    q = _unshift(qlo_ref[...], qhi_ref[...])
    q = (q.astype(jnp.float32) * scale).astype(jnp.bfloat16)
    crow = crow_ref[0, pl.ds(h, 1), :]
    cc = ccol_ref[...]
    lane = lax.broadcasted_iota(jnp.int32, cc.shape, 1)
    ccq = jnp.sum(jnp.where(lane == h + FOX_SHIFT, cc, 0.0), axis=1, keepdims=True)
    _scores(q, k_ref, s_ref, bias_fn=lambda c: ccq[c * TQ:, :] - crow[:, c * TQ:(c + 1) * TQ])
    for i in range(o_ref.shape[0] // TQ):
        _softmax_tile(s_ref, p_ref, i)
        o_ref[i * TQ:(i + 1) * TQ, :] = _attend(p_ref, va_ref, i).astype(o_ref.dtype)


def _diff_attn_kernel(lambda_init, q_ref, k_ref, v_ref, lam_ref, g_ref, o_ref, s_ref, p_ref, va_ref, o1_ref):
    lv = lam_ref[...]
    lam = (jnp.exp(jnp.sum(lv[0:1] * lv[1:2], axis=1, keepdims=True))
           - jnp.exp(jnp.sum(lv[2:3] * lv[3:4], axis=1, keepdims=True)) + lambda_init)
    gain = g_ref[...] * (1.0 - lambda_init)
    q = q_ref[...]
    lane = lax.broadcasted_iota(jnp.int32, q.shape, 1)
    zero = jnp.zeros_like(q)
    nq = q_ref.shape[0] // TQ
    _fill_values(va_ref, v_ref[...])
    _scores(jnp.where(lane < DIFF_QK_DIM, q, zero), k_ref, s_ref)
    for i in range(nq):
        _softmax_tile(s_ref, p_ref, i)
        o1_ref[i * TQ:(i + 1) * TQ, :] = _attend(p_ref, va_ref, i)
    _scores(jnp.where(lane >= DIFF_QK_DIM, q, zero), k_ref, s_ref)
    for i in range(nq):
        rs = slice(i * TQ, (i + 1) * TQ)
        _softmax_tile(s_ref, p_ref, i)
        o = o1_ref[rs, :] - lam * _attend(p_ref, va_ref, i)
        o_ref[rs, :] = (o * _rms_scale(o, SUBLN_EPS) * gain).astype(o_ref.dtype)


def _attn_call(kernel, name, batch, seq, heads, ins, in_specs, extra_scratch=()):
    return pl.pallas_call(
        kernel,
        grid=(batch, heads),
        in_specs=in_specs,
        out_specs=pl.BlockSpec((seq, HEAD_DIM), lambda b, h: (b, h)),
        out_shape=jax.ShapeDtypeStruct((batch * seq, heads * HEAD_DIM), jnp.bfloat16),
        scratch_shapes=[pltpu.VMEM((seq, seq), jnp.float32), pltpu.VMEM((seq, seq), jnp.bfloat16),
                        pltpu.VMEM((seq, 2 * HEAD_DIM), jnp.bfloat16), *extra_scratch],
        compiler_params=_cparams(("parallel", "parallel")),
        name=name,
    )(*ins)


def _colblk(seq, width, col0):
    assert col0 % width == 0
    return pl.BlockSpec((seq, width), lambda b, h: (b, col0 // width + h))


def _diff_attn(l, z, lam, gain, lambda_init, batch, seq):
    kern = functools.partial(_diff_attn_kernel, lambda_init)
    specs = [_colblk(seq, HEAD_DIM, Z_AQ), _colblk(seq, HEAD_DIM, Z_AK), _colblk(seq, HEAD_DIM, Z_AV),
             _layer_spec(l, (4, DIFF_QK_DIM)), _layer_spec(l, (1, HEAD_DIM))]
    return _attn_call(kern, "diff_attn", batch, seq, DIFF_HEADS, (z, z, z, lam, gain), specs,
                      extra_scratch=(pltpu.VMEM((seq, HEAD_DIM), jnp.float32),))


def _mla_attn(q, k, v, batch, seq):
    specs = [_colblk(seq, MLA_QK_PAD, 0), _colblk(seq, MLA_QK_PAD, 0), _colblk(seq, HEAD_DIM, 0)]
    return _attn_call(_mla_attn_kernel, "mla_attn", batch, seq, MLA_HEADS, (q, k, v), specs)


def _fox_attn(z, ccol, crow, batch, seq):
    specs = []
    for col0 in (Z_FQ, Z_FK, Z_FV):
        specs += [_colblk(seq, LANES, col0 - FOX_SHIFT), _colblk(seq, LANES, col0 - FOX_SHIFT + LANES)]
    specs += [pl.BlockSpec((seq, LANES), lambda b, h: (b, 0)),
              pl.BlockSpec((1, SUBLANES, seq), lambda b, h: (b, 0, 0))]
    kv = pltpu.VMEM((seq, HEAD_DIM), jnp.bfloat16)
    return _attn_call(_fox_attn_kernel, "fox_attn", batch, seq, FOX_HEADS, (z,) * 6 + (ccol, crow), specs,
                      extra_scratch=(kv,))


def _out_proj_kernel(x_ref, a_ref, b_ref, c_ref, w_ref, gn_ref, o_ref, hn_ref, wb_ref):
    @pl.when(pl.program_id(0) == 0)
    def _():
        wb_ref[...] = w_ref[...].astype(jnp.bfloat16)

    na, nb = a_ref.shape[1], b_ref.shape[1]
    acc = jnp.dot(a_ref[...], wb_ref[:na, :], preferred_element_type=jnp.float32)
    acc = acc + jnp.dot(b_ref[...], wb_ref[na:na + nb, :], preferred_element_type=jnp.float32)
    acc = acc + jnp.dot(c_ref[...], wb_ref[na + nb:, :], preferred_element_type=jnp.float32)
    xn = x_ref[...] + acc
    o_ref[...] = xn
    hn_ref[...] = (xn * _rms_scale(xn, NORM_EPS) * gn_ref[...]).astype(jnp.bfloat16)


def _out_proj(l, x, oa, ob, oc, w_o, ffn_gain):
    m, d = x.shape
    dm = w_o.shape[1]
    assert oa.shape[1] + ob.shape[1] + oc.shape[1] == dm
    lhs = lambda a: pl.BlockSpec((TM_OUT, a.shape[1]), lambda i: (i, 0))
    xo = pl.BlockSpec((TM_OUT, d), lambda i: (i, 0))
    return pl.pallas_call(
        _out_proj_kernel,
        grid=(m // TM_OUT,),
        in_specs=[xo, lhs(oa), lhs(ob), lhs(oc),
                  pl.BlockSpec((None, dm, d), lambda i: (l, 0, 0), pipeline_mode=pl.Buffered(1)),
                  _layer_spec(l, (1, d))],
        out_specs=[xo, xo],
        out_shape=[jax.ShapeDtypeStruct((m, d), jnp.float32), jax.ShapeDtypeStruct((m, d), jnp.bfloat16)],
        scratch_shapes=[pltpu.VMEM((dm, d), jnp.bfloat16)],
        compiler_params=_cparams(("arbitrary",)),
        name="out_proj",
    )(x, oa, ob, oc, w_o, ffn_gain)


def _ffn_up_kernel(tiles_per_seq, h_ref, wa_ref, wg_ref, cwa_ref, cwg_ref, cba_ref, cbg_ref,
                   o_ref, ua_ref, ug_ref, carry_ref):
    i, j = pl.program_id(0), pl.program_id(1)
    tm = h_ref.shape[0]

    @pl.when(i % tiles_per_seq == 0)
    def _():
        carry_ref[j] = jnp.zeros(carry_ref.shape[1:], jnp.float32)

    ua_ref[0:SUBLANES, :] = carry_ref[j, 0]
    ug_ref[0:SUBLANES, :] = carry_ref[j, 1]

    def conv(u_ref, w_ref, half, cw_ref, cb_ref):
        u = jnp.dot(h_ref[...], w_ref[...].astype(jnp.bfloat16), preferred_element_type=jnp.float32)
        u_ref[SUBLANES:, :] = u
        carry_ref[j, half] = u[tm - SUBLANES:, :]
        cw = cw_ref[...]
        return (u * cw[2:3] + u_ref[SUBLANES - 1:SUBLANES - 1 + tm, :] * cw[1:2]
                + u_ref[SUBLANES - 2:SUBLANES - 2 + tm, :] * cw[0:1] + cb_ref[...])

    g = conv(ug_ref, wg_ref, 1, cwg_ref, cbg_ref)
    gate = g / (1.0 + jnp.exp(-g))
    a = conv(ua_ref, wa_ref, 0, cwa_ref, cba_ref)
    o_ref[...] = (gate * a).astype(o_ref.dtype)


def _ffn_up(l, h, w_up, conv_w, conv_b, seq):
    m, d = h.shape
    nj = D_FF // TN_FF
    kern = functools.partial(_ffn_up_kernel, seq // TM_PROJ)
    col = lambda rows, off: _layer_spec(l, (rows, TN_FF), lambda i, j: j + off)
    return pl.pallas_call(
        kern,
        grid=(m // TM_PROJ, nj),
        in_specs=[pl.BlockSpec((TM_PROJ, d), lambda i, j: (i, 0)),
                  col(d, 0), col(d, nj), col(3, 0), col(3, nj), col(1, 0), col(1, nj)],
        out_specs=pl.BlockSpec((TM_PROJ, TN_FF), lambda i, j: (i, j)),
        out_shape=jax.ShapeDtypeStruct((m, D_FF), jnp.bfloat16),
        scratch_shapes=[pltpu.VMEM((TM_PROJ + SUBLANES, TN_FF), jnp.float32),
                        pltpu.VMEM((TM_PROJ + SUBLANES, TN_FF), jnp.float32),
                        pltpu.VMEM((nj, 2, SUBLANES, TN_FF), jnp.float32)],
        compiler_params=_cparams(("arbitrary", "arbitrary")),
        name="ffn_up",
    )(h, w_up, w_up, conv_w, conv_w, conv_b, conv_b)


def _ffn_down_kernel(x_ref, a_ref, w_ref, o_ref):
    o_ref[...] = x_ref[...] + jnp.dot(a_ref[...], w_ref[...], preferred_element_type=jnp.float32)


def _ffn_down(l, x, act, w):
    m, d = x.shape
    k = act.shape[1]
    xo = pl.BlockSpec((TM_PROJ, TN_DOWN), lambda i, j: (i, j))
    return pl.pallas_call(
        _ffn_down_kernel,
        grid=(m // TM_PROJ, d // TN_DOWN),
        in_specs=[xo, pl.BlockSpec((TM_PROJ, k), lambda i, j: (i, 0)),
                  _layer_spec(l, (k, TN_DOWN), lambda i, j: j)],
        out_specs=xo,
        out_shape=jax.ShapeDtypeStruct((m, d), jnp.float32),
        compiler_params=_cparams(("parallel", "parallel")),
        name="ffn_down",
    )(x, act, w)


def _final_norm_kernel(x_ref, g_ref, o_ref):
    x = x_ref[...]
    o_ref[...] = x * _rms_scale(x, NORM_EPS) * g_ref[...]


def _final_norm(x, gain):
    m, d = x.shape
    blk = pl.BlockSpec((TM_PREP, d), lambda i: (i, 0))
    return pl.pallas_call(
        _final_norm_kernel,
        grid=(m // TM_PREP,),
        in_specs=[blk, pl.BlockSpec((1, d), lambda i: (0, 0))],
        out_specs=blk,
        out_shape=jax.ShapeDtypeStruct((m, d), jnp.float32),
        compiler_params=_cparams(("parallel",)),
        name="final_norm",
    )(x, gain)


def _w_in_prep_kernel(w_ref, t_ref, o_ref):
    c = pl.program_id(0)
    n_main = pl.num_programs(0) - t_ref.shape[2] // LANES

    @pl.when(c < n_main)
    def _():
        for l in range(o_ref.shape[0]):
            o_ref[l] = w_ref[:, l, :].T.astype(jnp.bfloat16)

    for t in range(t_ref.shape[2] // LANES):
        @pl.when(c == n_main + t)
        def _():
            o_ref[...] = t_ref[:, :, t * LANES:(t + 1) * LANES]


def _w_in_prep(w, width):
    dp, k, n = w.shape
    n_main = n // LANES
    tail_w = width - n_main * LANES
    assert width % LANES == 0 and tail_w >= n - n_main * LANES
    wt = jnp.transpose(w, (2, 0, 1))
    tail = jnp.pad(w[:, :, n_main * LANES:], ((0, 0), (0, 0), (0, tail_w - (n - n_main * LANES))))
    return pl.pallas_call(
        _w_in_prep_kernel,
        grid=(width // LANES,),
        in_specs=[pl.BlockSpec((LANES, dp, k), lambda c: (jnp.minimum(c, n_main - 1), 0, 0)),
                  pl.BlockSpec((dp, k, tail_w), lambda c: (0, 0, 0))],
        out_specs=pl.BlockSpec((dp, k, LANES), lambda c: (0, 0, c)),
        out_shape=jax.ShapeDtypeStruct((dp, k, width), jnp.bfloat16),
        compiler_params=_cparams(("arbitrary",)),
        name="w_in_prep",
    )(wt, tail.astype(jnp.bfloat16))


def _pad_w_uq(w):
    dp, r, _ = w.shape
    w = w.reshape(dp, r, MLA_HEADS, MLA_QK_NOPE + MLA_QK_ROPE)
    w = jnp.pad(w, ((0, 0), (0, 0), (0, 0), (0, MLA_QK_PAD - MLA_QK_NOPE - MLA_QK_ROPE)))
    return w.reshape(dp, r, MLA_HEADS * MLA_QK_PAD).astype(jnp.bfloat16)


def _split_w_ukv(w):
    dp, r, _ = w.shape
    w = w.reshape(dp, r, MLA_HEADS, MLA_QK_NOPE + HEAD_DIM)
    wk = w[..., :MLA_QK_NOPE].reshape(dp, r, MLA_HEADS * MLA_QK_NOPE)
    wv = w[..., MLA_QK_NOPE:].reshape(dp, r, MLA_HEADS * HEAD_DIM)
    return wk.astype(jnp.bfloat16), wv.astype(jnp.bfloat16)


def kernel(x, positions, attn_norm, w_in, diff_lambda, diff_out_norm, mla_q_norm, mla_kv_norm, mla_w_uq,
           mla_w_ukv, fox_forget_bias, w_o, ffn_norm, ffn_w_up, ffn_conv_w, ffn_conv_b, ffn_w_down,
           final_norm):
    batch, seq, d = x.shape
    depth = w_in.shape[0]
    m = batch * seq
    assert seq % TQ == 0 and seq % TM_PROJ == 0 and m % TM_PROJ == 0
    assert w_in.shape[2] == IN_WIDTH

    pat_a, pat_b = _rope_lane_patterns()
    pos_col = positions.reshape(m, 1)
    tabs_a = _rope_tables(pos_col, *pat_a)
    tabs_b = _rope_tables(pos_col, *pat_b)

    bf = lambda a: a.astype(jnp.bfloat16)
    rows = lambda a: a.reshape(depth, 1, -1)
    w_in_b = _w_in_prep(w_in, Z_WIDTH)
    w_uq_b = _pad_w_uq(mla_w_uq)
    w_uk_b, w_uv_b = _split_w_ukv(mla_w_ukv)
    w_down_b = bf(ffn_w_down)
    fb = jnp.pad(fox_forget_bias, ((0, 0), (FOX_SHIFT, LANES - FOX_SHIFT - FOX_HEADS))).reshape(depth, 1, LANES)
    g_attn, g_ffn, g_q, g_kv, g_diff = (rows(attn_norm), rows(ffn_norm), rows(mla_q_norm),
                                        rows(mla_kv_norm), rows(diff_out_norm))
    conv_b = rows(ffn_conv_b)

    xf = x.reshape(m, d)
    for l in range(depth):
        lambda_init = 0.8 - 0.6 * math.exp(-0.3 * l)
        z, zg = _in_proj(l, xf, g_attn, w_in_b, tabs_a)
        q_b, k_b, v_b = _mla_up(l, z, g_q, g_kv, w_uq_b, w_uk_b, w_uv_b, tabs_b)
        ccol, crow = _fox_gate(l, zg, fb, batch, seq)
        o_a = _diff_attn(l, z, diff_lambda, g_diff, lambda_init, batch, seq)
        o_b = _mla_attn(q_b, k_b, v_b, batch, seq)
        o_c = _fox_attn(z, ccol, crow, batch, seq)
        xf, h_ffn = _out_proj(l, xf, o_a, o_b, o_c, w_o, g_ffn)
        act = _ffn_up(l, h_ffn, ffn_w_up, ffn_conv_w, conv_b, seq)
        xf = _ffn_down(l, xf, act, w_down_b)
    return _final_norm(xf, final_norm.reshape(1, d)).reshape(batch, seq, d)
```
